```python
import math
import jax, jax.numpy as jnp
from jax import lax
import numpy as np

D_MODEL = 1024
BATCH = 4
SEQ = 8192
DEPTH = 2

N_MIXERS = 2
N_S5_LAYERS = (DEPTH + 1) // 2
N_ATTN_LAYERS = DEPTH // 2

S5_GROUP = 16
S5_GROUPS = D_MODEL // S5_GROUP
S5_STATE = 64
DT_MIN = 1e-3
DT_MAX = 1e-1

HEAD_DIM = 64
N_Q_HEADS = D_MODEL // HEAD_DIM
N_KV_HEADS = 4
Q_PER_KV = N_Q_HEADS // N_KV_HEADS
WINDOW = 128
BLOCK = 128
ROPE_DIM = HEAD_DIM // 4
ROPE_THETA = 500000.0
QKV_DIM = (N_Q_HEADS + 2 * N_KV_HEADS) * HEAD_DIM
NEG_INF = -1e30

D_FF = 2816
CONV_WIDTH = 3

EPS = 1e-5

kernel_name = "hybrid_s5_swa_sink_convffn"


def rmsnorm(x, g):
    xf = x.astype(jnp.float32)
    y = xf * lax.rsqrt(jnp.mean(xf * xf, axis=-1, keepdims=True) + EPS)
    return (y * g.astype(jnp.float32)).astype(x.dtype)


def s5_mixer(u, lam_re, lam_im, log_dt, b_re, b_im, c_re, c_im, d_skip, w_glu, b_glu):
    f32 = jnp.float32
    bsz, L, d = u.shape
    lr, li = lam_re.astype(f32), lam_im.astype(f32)
    dt = jnp.exp(log_dt.astype(f32))[:, None]
    mag = jnp.exp(lr * dt)
    a_re = mag * jnp.cos(li * dt)
    a_im = mag * jnp.sin(li * dt)
    den = lr * lr + li * li
    z_re = ((a_re - 1.0) * lr + a_im * li) / den
    z_im = (a_im * lr - (a_re - 1.0) * li) / den
    br, bi = b_re.astype(f32), b_im.astype(f32)
    bb_re = z_re[..., None] * br - z_im[..., None] * bi
    bb_im = z_re[..., None] * bi + z_im[..., None] * br
    ug = u.astype(f32).reshape(bsz, L, S5_GROUPS, S5_GROUP)
    bu_re = jnp.einsum('blgc,gpc->lbgp', ug, bb_re)
    bu_im = jnp.einsum('blgc,gpc->lbgp', ug, bb_im)
    a_re_t = jnp.broadcast_to(a_re[None, None], (L, 1, S5_GROUPS, S5_STATE))
    a_im_t = jnp.broadcast_to(a_im[None, None], (L, 1, S5_GROUPS, S5_STATE))

    def combine(e1, e2):
        a1r, a1i, b1r, b1i = e1
        a2r, a2i, b2r, b2i = e2
        return (a2r * a1r - a2i * a1i,
                a2r * a1i + a2i * a1r,
                a2r * b1r - a2i * b1i + b2r,
                a2r * b1i + a2i * b1r + b2i)

    _, _, h_re, h_im = lax.associative_scan(combine, (a_re_t, a_im_t, bu_re, bu_im), axis=0)
    y = (jnp.einsum('lbgp,gcp->blgc', h_re, c_re.astype(f32))
         - jnp.einsum('lbgp,gcp->blgc', h_im, c_im.astype(f32)))
    y = y.reshape(bsz, L, d) + d_skip.astype(f32) * u.astype(f32)
    g = jax.nn.gelu(y).astype(u.dtype)
    return g * jax.nn.sigmoid(g @ w_glu + b_glu)


def rope_partial(t, pos):
    f32 = jnp.float32
    half = ROPE_DIM // 2
    inv_freq = 1.0 / jnp.power(ROPE_THETA, jnp.arange(0, ROPE_DIM, 2, dtype=f32) / ROPE_DIM)
    ang = pos.astype(f32)[..., None] * inv_freq
    cos = jnp.cos(ang)[:, :, None, :]
    sin = jnp.sin(ang)[:, :, None, :]
    tr = t[..., :ROPE_DIM].astype(f32)
    t1, t2 = tr[..., :half], tr[..., half:]
    rot = jnp.concatenate([t1 * cos - t2 * sin, t2 * cos + t1 * sin], axis=-1).astype(t.dtype)
    return jnp.concatenate([rot, t[..., ROPE_DIM:]], axis=-1)


def swa_mixer(h, pos, w_qkv, b_qkv, sinks, w_o, b_o):
    f32 = jnp.float32
    bsz, L, d = h.shape
    nb = L // BLOCK
    qkv = h @ w_qkv + b_qkv
    q, k, v = jnp.split(qkv, [N_Q_HEADS * HEAD_DIM, (N_Q_HEADS + N_KV_HEADS) * HEAD_DIM], axis=-1)
    q = rope_partial(q.reshape(bsz, L, N_Q_HEADS, HEAD_DIM), pos)
    k = rope_partial(k.reshape(bsz, L, N_KV_HEADS, HEAD_DIM), pos)
    v = v.reshape(bsz, L, N_KV_HEADS, HEAD_DIM)
    q = q.reshape(bsz, nb, BLOCK, N_KV_HEADS, Q_PER_KV, HEAD_DIM)
    k = k.reshape(bsz, nb, BLOCK, N_KV_HEADS, HEAD_DIM)
    v = v.reshape(bsz, nb, BLOCK, N_KV_HEADS, HEAD_DIM)

    def with_prev(t):
        prev = jnp.pad(t[:, :-1], ((0, 0), (1, 0), (0, 0), (0, 0), (0, 0)))
        return jnp.concatenate([prev, t], axis=2)

    kb, vb = with_prev(k), with_prev(v)
    s = jnp.einsum('bnqhgd,bnkhd->bhgnqk', q, kb).astype(f32) * (HEAD_DIM ** -0.5)
    qi = jnp.arange(BLOCK)[:, None] + BLOCK
    ki = jnp.arange(2 * BLOCK)[None, :]
    band = (ki <= qi) & (qi - ki < WINDOW)
    first = (jnp.arange(nb) == 0)[:, None, None]
    valid = band[None] & ~(first & (ki < BLOCK)[None])
    s = jnp.where(valid, s, NEG_INF)
    sink = sinks.astype(f32).reshape(N_KV_HEADS, Q_PER_KV)[None, :, :, None, None, None]
    m = jnp.maximum(jnp.max(s, axis=-1, keepdims=True), sink)
    p = jnp.exp(s - m)
    p = p / (jnp.sum(p, axis=-1, keepdims=True) + jnp.exp(sink - m))
    o = jnp.einsum('bhgnqk,bnkhd->bnqhgd', p.astype(vb.dtype), vb).reshape(bsz, L, d)
    return o @ w_o + b_o


def conv_ffn(h, w_up, w_conv, b_conv, w_down):
    u = h @ w_up
    ch = u.shape[-1]
    u = lax.conv_general_dilated(
        u, w_conv[:, None, :].astype(u.dtype), window_strides=(1,),
        padding=[(CONV_WIDTH - 1, 0)], dimension_numbers=('NWC', 'WIO', 'NWC'),
        feature_group_count=ch) + b_conv
    a, val = jnp.split(u, 2, axis=-1)
    return (jax.nn.silu(a) * val) @ w_down


def setup_inputs(seed: int = 0) -> dict:
    key = jax.random.key(seed)
    ks = iter(jax.random.split(key, 32))
    nrm = lambda shape, scale: jax.random.normal(next(ks), shape, jnp.float32) * scale
    G, P, C = S5_GROUPS, S5_STATE, S5_GROUP
    na, nbl = N_S5_LAYERS, N_ATTN_LAYERS
    x = jax.random.normal(next(ks), (BATCH, SEQ, D_MODEL), jnp.float32)
    offs = jax.random.randint(next(ks), (BATCH, 1), 0, 1024, dtype=jnp.int32)
    positions = offs + jnp.arange(SEQ, dtype=jnp.int32)[None, :]
    n_idx = jnp.arange(P, dtype=jnp.float32)
    return {
        "x": x,
        "positions": positions,
        "norm_mix": 1.0 + nrm((DEPTH, D_MODEL), 0.02),
        "norm_ffn": 1.0 + nrm((DEPTH, D_MODEL), 0.02),
        "norm_final": 1.0 + nrm((D_MODEL,), 0.02),
        "s5_lambda_re": -0.5 + nrm((na, G, P), 0.01),
        "s5_lambda_im": math.pi * n_idx + nrm((na, G, P), 0.01),
        "s5_log_dt": jax.random.uniform(next(ks), (na, G), jnp.float32, math.log(DT_MIN), math.log(DT_MAX)),
        "s5_b_re": nrm((na, G, P, C), (2 * C) ** -0.5),
        "s5_b_im": nrm((na, G, P, C), (2 * C) ** -0.5),
        "s5_c_re": nrm((na, G, C, P), (2 * P) ** -0.5),
        "s5_c_im": nrm((na, G, C, P), (2 * P) ** -0.5),
        "s5_d": nrm((na, D_MODEL), 1.0),
        "s5_w_glu": nrm((na, D_MODEL, D_MODEL), D_MODEL ** -0.5),
        "s5_b_glu": nrm((na, D_MODEL), 0.01),
        "attn_w_qkv": nrm((nbl, D_MODEL, QKV_DIM), D_MODEL ** -0.5),
        "attn_b_qkv": nrm((nbl, QKV_DIM), 0.01),
        "attn_sinks": nrm((nbl, N_Q_HEADS), 0.5),
        "attn_w_o": nrm((nbl, D_MODEL, D_MODEL), D_MODEL ** -0.5),
        "attn_b_o": nrm((nbl, D_MODEL), 0.01),
        "ffn_w_up": nrm((DEPTH, D_MODEL, 2 * D_FF), D_MODEL ** -0.5),
        "ffn_w_conv": nrm((DEPTH, CONV_WIDTH, 2 * D_FF), CONV_WIDTH ** -0.5),
        "ffn_b_conv": nrm((DEPTH, 2 * D_FF), 0.01),
        "ffn_w_down": nrm((DEPTH, D_FF, D_MODEL), D_FF ** -0.5),
    }


def reference(x, positions, norm_mix, norm_ffn, norm_final,
              s5_lambda_re, s5_lambda_im, s5_log_dt, s5_b_re, s5_b_im, s5_c_re, s5_c_im,
              s5_d, s5_w_glu, s5_b_glu,
              attn_w_qkv, attn_b_qkv, attn_sinks, attn_w_o, attn_b_o,
              ffn_w_up, ffn_w_conv, ffn_b_conv, ffn_w_down):
    for i in range(DEPTH):
        j = i // N_MIXERS
        h = rmsnorm(x, norm_mix[i])
        if i % N_MIXERS == 0:
            x = x + s5_mixer(h, s5_lambda_re[j], s5_lambda_im[j], s5_log_dt[j],
                             s5_b_re[j], s5_b_im[j], s5_c_re[j], s5_c_im[j],
                             s5_d[j], s5_w_glu[j], s5_b_glu[j])
        else:
            x = x + swa_mixer(h, positions, attn_w_qkv[j], attn_b_qkv[j],
                              attn_sinks[j], attn_w_o[j], attn_b_o[j])
        h = rmsnorm(x, norm_ffn[i])
        x = x + conv_ffn(h, ffn_w_up[i], ffn_w_conv[i], ffn_b_conv[i], ffn_w_down[i])
    return rmsnorm(x, norm_final)
```

```python
import functools

import jax
import jax.numpy as jnp
from jax import lax
from jax.experimental import pallas as pl
from jax.experimental.pallas import tpu as pltpu

F32 = jnp.float32
BF16 = jnp.bfloat16

EPS = 1e-5
NEG_INF = -1e30

HEAD_DIM = 64
N_KV_HEADS = 4
ROPE_DIM = 16
ROPE_THETA = 500000.0
ATTN_BLOCK = 128
S5_GROUP = 16

LANES = 128
SUBLANES = 8
MXU_DIM = 256
VMEM_LIMIT = 56 * 1024 * 1024

Q = 16
NGB = 8
GPB = LANES // S5_GROUP
SEG = 64
NSEG = 8
PITCH = 72


def _rms(x, g):
    ms = jnp.mean(x * x, axis=-1, keepdims=True)
    return x * lax.rsqrt(ms + EPS) * g


def _const_spec(shape):
    nd = len(shape)
    return pl.BlockSpec(shape, lambda *_: (0,) * nd, pipeline_mode=pl.Buffered(1))


def _ffn_kernel(x_ref, g_ref, wu_ref, wc_ref, bc_ref, wd_ref, gf_ref, o_ref, act_ref, carry_ref,
                *, fb, final_norm):
    tm = x_ref.shape[1]
    d_ff = wd_ref.shape[0]

    @pl.when(pl.program_id(1) == 0)
    def _():
        carry_ref[...] = jnp.zeros_like(carry_ref)

    x = x_ref[0]
    h = _rms(x, g_ref[...]).astype(BF16)
    rows = lax.broadcasted_iota(jnp.int32, (SUBLANES, fb), 0)

    def conv_block(col):
        u = jnp.dot(h, wu_ref[:, col:col + fb], preferred_element_type=F32)
        prev = carry_ref[:, col:col + fb]
        carry_ref[:, col:col + fb] = u[tm - SUBLANES:, :]
        s1 = pltpu.roll(u, 1, 0)
        s2 = pltpu.roll(u, 2, 0)
        t1 = jnp.where(rows < 1, pltpu.roll(prev, 1, 0), s1[:SUBLANES])
        t2 = jnp.where(rows < 2, pltpu.roll(prev, 2, 0), s2[:SUBLANES])
        s1 = jnp.concatenate([t1, s1[SUBLANES:]], axis=0)
        s2 = jnp.concatenate([t2, s2[SUBLANES:]], axis=0)
        w = wc_ref[:, col:col + fb]
        return w[0:1] * s2 + w[1:2] * s1 + w[2:3] * u + bc_ref[:, col:col + fb]

    for j in range(d_ff // fb):
        a = conv_block(j * fb)
        v = conv_block(d_ff + j * fb)
        act_ref[:, j * fb:(j + 1) * fb] = (a * (1.0 / (1.0 + jnp.exp(-a))) * v).astype(BF16)

    y = jnp.dot(act_ref[...], wd_ref[...], preferred_element_type=F32) + x
    if final_norm:
        y = _rms(y, gf_ref[...])
    o_ref[0] = y


def _conv_ffn(x, g, w_up, w_conv, b_conv, w_down, g_final, *, tm=512, fb=256):
    bsz, L, d = x.shape
    d_ff = w_down.shape[0]
    final_norm = g_final is not None
    gf = g_final if final_norm else g
    kern = functools.partial(_ffn_kernel, fb=fb, final_norm=final_norm)
    return pl.pallas_call(
        kern,
        grid=(bsz, L // tm),
        in_specs=[
            pl.BlockSpec((1, tm, d), lambda b, i: (b, i, 0)),
            _const_spec((1, d)),
            _const_spec((d, 2 * d_ff)),
            _const_spec((3, 2 * d_ff)),
            _const_spec((1, 2 * d_ff)),
            _const_spec((d_ff, d)),
            _const_spec((1, d)),
        ],
        out_specs=pl.BlockSpec((1, tm, d), lambda b, i: (b, i, 0)),
        out_shape=jax.ShapeDtypeStruct(x.shape, F32),
        scratch_shapes=[
            pltpu.VMEM((tm, d_ff), BF16),
            pltpu.VMEM((SUBLANES, 2 * d_ff), F32),
        ],
        compiler_params=pltpu.CompilerParams(
            dimension_semantics=("arbitrary", "arbitrary"), vmem_limit_bytes=VMEM_LIMIT),
        name="conv_ffn",
    )(x, g.reshape(1, d), w_up.astype(BF16), w_conv, b_conv.reshape(1, -1), w_down.astype(BF16),
      gf.reshape(1, d))


def _qkv_kernel(x_ref, pos_ref, g_ref, w_ref, b_ref, fr_ref, sg_ref, q_ref, kv_ref):
    d = x_ref.shape[2]
    n_rot = (d + N_KV_HEADS * HEAD_DIM) // LANES
    h = _rms(x_ref[0], g_ref[...]).astype(BF16)
    qkv = jnp.dot(h, w_ref[...], preferred_element_type=F32) + b_ref[...]
    ang = pos_ref[0].astype(F32) * fr_ref[...]
    cos = jnp.cos(ang)
    sin = jnp.sin(ang) * sg_ref[...]
    lane = lax.broadcasted_iota(jnp.int32, ang.shape, 1)
    low_half = (lane & (HEAD_DIM - 1)) < (ROPE_DIM // 2)
    for cb in range(n_rot):
        t = qkv[:, cb * LANES:(cb + 1) * LANES]
        partner = jnp.where(low_half, pltpu.roll(t, LANES - ROPE_DIM // 2, 1), pltpu.roll(t, ROPE_DIM // 2, 1))
        o = t * cos + partner * sin
        if cb * LANES < d:
            q_ref[0, :, cb * LANES:(cb + 1) * LANES] = (o * (HEAD_DIM ** -0.5)).astype(BF16)
        else:
            kv_ref[0, :, cb * LANES - d:(cb + 1) * LANES - d] = o.astype(BF16)
    nkv = N_KV_HEADS * HEAD_DIM
    kv_ref[0, :, nkv:] = qkv[:, d + nkv:].astype(BF16)


def _attn_kernel(sink_ref, q_ref, kvc_ref, kvp_ref, x_ref, wo_ref, bo_ref, o_ref, oh_ref):
    tq = q_ref.shape[1]
    blk = ATTN_BLOCK
    nkv = N_KV_HEADS * HEAD_DIM
    q_per_kv = q_ref.shape[2] // nkv
    first_tile = pl.program_id(1) == 0
    qi = lax.broadcasted_iota(jnp.int32, (blk, blk), 0)
    ki = lax.broadcasted_iota(jnp.int32, (blk, blk), 1)
    cur_valid = ki <= qi
    prev_valid = ki > qi
    dn = (((1,), (1,)), ((), ()))
    for n in range(tq // blk):
        rows = slice(n * blk, (n + 1) * blk)
        kv_c = kvc_ref[0, rows, :]
        if n == 0:
            kv_p = kvp_ref[0]
            pmask = jnp.logical_and(prev_valid, jnp.logical_not(first_tile))
        else:
            kv_p = kvc_ref[0, (n - 1) * blk:n * blk, :]
            pmask = prev_valid
        for hk in range(N_KV_HEADS):
            k_c = kv_c[:, hk * HEAD_DIM:(hk + 1) * HEAD_DIM]
            k_p = kv_p[:, hk * HEAD_DIM:(hk + 1) * HEAD_DIM]
            v_c = kv_c[:, nkv + hk * HEAD_DIM:nkv + (hk + 1) * HEAD_DIM]
            v_p = kv_p[:, nkv + hk * HEAD_DIM:nkv + (hk + 1) * HEAD_DIM]
            for g in range(q_per_kv):
                hq = hk * q_per_kv + g
                cols = slice(hq * HEAD_DIM, (hq + 1) * HEAD_DIM)
                qh = q_ref[0, rows, cols]
                sink = sink_ref[hq]
                s_c = jnp.where(cur_valid, lax.dot_general(qh, k_c, dn, preferred_element_type=F32), NEG_INF)
                s_p = jnp.where(pmask, lax.dot_general(qh, k_p, dn, preferred_element_type=F32), NEG_INF)
                m = jnp.maximum(jnp.maximum(jnp.max(s_c, axis=-1, keepdims=True),
                                            jnp.max(s_p, axis=-1, keepdims=True)), sink)
                p_c = jnp.exp(s_c - m)
                p_p = jnp.exp(s_p - m)
                den = (jnp.sum(p_c, axis=-1, keepdims=True) + jnp.sum(p_p, axis=-1, keepdims=True)
                       + jnp.exp(sink - m))
                o = (jnp.dot(p_c.astype(BF16), v_c, preferred_element_type=F32)
                     + jnp.dot(p_p.astype(BF16), v_p, preferred_element_type=F32))
                oh_ref[rows, cols] = (o / den).astype(BF16)
    o_ref[0] = jnp.dot(oh_ref[...], wo_ref[...], preferred_element_type=F32) + bo_ref[...] + x_ref[0]


def _swa_layer(x, pos, g, w_qkv, b_qkv, sinks, w_o, b_o, *, tm=512, tq=512):
    bsz, L, d = x.shape
    nkv = N_KV_HEADS * HEAD_DIM
    qkv_dim = w_qkv.shape[1]
    half = ROPE_DIM // 2
    inv_freq = 1.0 / jnp.power(ROPE_THETA, jnp.arange(0, ROPE_DIM, 2, dtype=F32) / ROPE_DIM)
    dim = jnp.arange(LANES) % HEAD_DIM
    freq = jnp.where(dim < ROPE_DIM, inv_freq[dim % half], 0.0).reshape(1, LANES)
    sign = jnp.where(dim < half, -1.0, 1.0).astype(F32).reshape(1, LANES)

    q, kv = pl.pallas_call(
        _qkv_kernel,
        grid=(bsz, L // tm),
        in_specs=[
            pl.BlockSpec((1, tm, d), lambda b, i: (b, i, 0)),
            pl.BlockSpec((1, tm, 1), lambda b, i: (b, i, 0)),
            _const_spec((1, d)),
            _const_spec((d, qkv_dim)),
            _const_spec((1, qkv_dim)),
            _const_spec((1, LANES)),
            _const_spec((1, LANES)),
        ],
        out_specs=[
            pl.BlockSpec((1, tm, d), lambda b, i: (b, i, 0)),
            pl.BlockSpec((1, tm, 2 * nkv), lambda b, i: (b, i, 0)),
        ],
        out_shape=[
            jax.ShapeDtypeStruct((bsz, L, d), BF16),
            jax.ShapeDtypeStruct((bsz, L, 2 * nkv), BF16),
        ],
        compiler_params=pltpu.CompilerParams(
            dimension_semantics=("arbitrary", "arbitrary"), vmem_limit_bytes=VMEM_LIMIT),
        name="qkv_rope",
    )(x, pos.reshape(bsz, L, 1), g.reshape(1, d), w_qkv.astype(BF16), b_qkv.reshape(1, -1), freq, sign)

    bpt = tq // ATTN_BLOCK
    return pl.pallas_call(
        _attn_kernel,
        grid=(bsz, L // tq),
        in_specs=[
            pl.BlockSpec(memory_space=pltpu.SMEM),
            pl.BlockSpec((1, tq, d), lambda b, i: (b, i, 0)),
            pl.BlockSpec((1, tq, 2 * nkv), lambda b, i: (b, i, 0)),
            pl.BlockSpec((1, ATTN_BLOCK, 2 * nkv), lambda b, i: (b, jnp.maximum(i * bpt - 1, 0), 0)),
            pl.BlockSpec((1, tq, d), lambda b, i: (b, i, 0)),
            _const_spec((d, d)),
            _const_spec((1, d)),
        ],
        out_specs=pl.BlockSpec((1, tq, d), lambda b, i: (b, i, 0)),
        out_shape=jax.ShapeDtypeStruct(x.shape, F32),
        scratch_shapes=[pltpu.VMEM((tq, d), BF16)],
        compiler_params=pltpu.CompilerParams(
            dimension_semantics=("arbitrary", "arbitrary"), vmem_limit_bytes=VMEM_LIMIT),
        name="swa_attn",
    )(sinks.astype(F32), q, kv, kv, x, w_o.astype(BF16), b_o.reshape(1, d))


def _s5_norm_kernel(x_ref, g_ref, h_ref):
    d = g_ref.shape[1]
    for s in range(Q):
        hs = _rms(x_ref[:, s * d:(s + 1) * d], g_ref[...]).astype(BF16)
        for gb in range(NGB):
            h_ref[gb, :, s * LANES:(s + 1) * LANES] = hs[:, gb * LANES:(gb + 1) * LANES]


def _s5_core_kernel(h_ref, w_ref, r_ref, v_ref, cst_ref, pw_ref, y_ref, z_ref, sx_ref):
    nslab = z_ref.shape[0]
    nc = nslab // 2
    lhs = h_ref[0]
    z = jnp.dot(lhs, w_ref[0], preferred_element_type=F32)
    for j in range(NSEG):
        for c in range(nslab):
            z_ref[c, j * PITCH:j * PITCH + SEG, :] = z[j * SEG:(j + 1) * SEG, c * LANES:(c + 1) * LANES]

    a_re = [cst_ref[0, c, 0:SUBLANES, :] for c in range(nc)]
    a_im = [cst_ref[0, c + nc, 0:SUBLANES, :] for c in range(nc)]

    def scan_step(t, state):
        new_re, new_im = [], []
        for c in range(nc):
            s_re, s_im = state[c], state[c + nc]
            sx_ref[c, pl.ds(t, NSEG, stride=PITCH), :] = s_re
            sx_ref[c + nc, pl.ds(t, NSEG, stride=PITCH), :] = s_im
            z_re = z_ref[c, pl.ds(t, NSEG, stride=PITCH), :]
            z_im = z_ref[c + nc, pl.ds(t, NSEG, stride=PITCH), :]
            new_re.append(a_re[c] * s_re - a_im[c] * s_im + z_re)
            new_im.append(a_re[c] * s_im + a_im[c] * s_re + z_im)
        return tuple(new_re + new_im)

    zero = jnp.zeros((NSEG, LANES), F32)
    end = lax.fori_loop(0, SEG, scan_step, (zero,) * nslab)

    for c in range(nc):
        ap_re = cst_ref[0, c, SUBLANES:SUBLANES + 1, :]
        ap_im = cst_ref[0, c + nc, SUBLANES:SUBLANES + 1, :]
        c_re = jnp.zeros((1, LANES), F32)
        c_im = jnp.zeros((1, LANES), F32)
        for j in range(1, NSEG):
            e_re = end[c][j - 1:j]
            e_im = end[c + nc][j - 1:j]
            c_re, c_im = (e_re + ap_re * c_re - ap_im * c_im, e_im + ap_re * c_im + ap_im * c_re)
            p_re = pw_ref[0, c]
            p_im = pw_ref[0, c + nc]
            rows = slice(j * PITCH, j * PITCH + SEG)
            sx_ref[c, rows, :] = sx_ref[c, rows, :] + (p_re * c_re - p_im * c_im)
            sx_ref[c + nc, rows, :] = sx_ref[c + nc, rows, :] + (p_re * c_im + p_im * c_re)

    sx = jnp.concatenate(
        [jnp.concatenate([sx_ref[c, j * PITCH:j * PITCH + SEG, :] for c in range(nslab)], axis=1)
         for j in range(NSEG)], axis=0).astype(BF16)

    nt = lhs.shape[1] // MXU_DIM
    for b in range(nt):
        acc = jnp.dot(lhs[:, :(b + 1) * MXU_DIM], r_ref[0, (nt - 1 - b) * MXU_DIM:, :],
                      preferred_element_type=F32)
        acc = acc + jnp.dot(sx, v_ref[0, :, b * MXU_DIM:(b + 1) * MXU_DIM], preferred_element_type=F32)
        y_ref[0, :, b * MXU_DIM:(b + 1) * MXU_DIM] = acc.astype(BF16)


def _s5_out_kernel(x_ref, y_ref, w_ref, b_ref, o_ref, g_ref):
    mc = x_ref.shape[0]
    d = w_ref.shape[0]
    c0 = 0.7978845608028654
    for s in range(Q):
        for gb in range(NGB):
            y = y_ref[gb, :, s * LANES:(s + 1) * LANES].astype(F32)
            g_ref[s * mc:(s + 1) * mc, gb * LANES:(gb + 1) * LANES] = (
                0.5 * y * (1.0 + jnp.tanh(c0 * (y + 0.044715 * (y * y * y)))))
    g = g_ref[...]
    gate = jnp.dot(g.astype(BF16), w_ref[...], preferred_element_type=F32) + b_ref[...]
    res = g * (1.0 / (1.0 + jnp.exp(-gate)))
    for s in range(Q):
        o_ref[:, s * d:(s + 1) * d] = x_ref[:, s * d:(s + 1) * d] + res[s * mc:(s + 1) * mc, :]


def _cmul(ar, ai, br, bi):
    return ar * br - ai * bi, ar * bi + ai * br


def _s5_tables(lam_re, lam_im, log_dt, b_re, b_im, c_re, c_im, d_skip):
    hi = lax.Precision.HIGHEST
    G, P = lam_re.shape
    C = b_re.shape[-1]
    lr, li = lam_re.astype(F32), lam_im.astype(F32)
    dt = jnp.exp(log_dt.astype(F32))[:, None]

    def apow(n):
        mag = jnp.exp(n * (lr * dt))
        return mag * jnp.cos(n * (li * dt)), mag * jnp.sin(n * (li * dt))

    a_re, a_im = apow(1.0)
    den = lr * lr + li * li
    z_re = ((a_re - 1.0) * lr + a_im * li) / den
    z_im = (a_im * lr - (a_re - 1.0) * li) / den
    br, bi = b_re.astype(F32), b_im.astype(F32)
    bb_re = z_re[..., None] * br - z_im[..., None] * bi
    bb_im = z_re[..., None] * bi + z_im[..., None] * br
    cr, ci = c_re.astype(F32), c_im.astype(F32)

    n = jnp.arange(Q + 1, dtype=F32)[:, None, None]
    pr, pi = apow(n)
    m_re, m_im = _cmul(cr[None], ci[None], pr[:, :, None, :], pi[:, :, None, :])
    k = (jnp.einsum('ngop,gpi->ngoi', m_re[:Q], bb_re, precision=hi)
         - jnp.einsum('ngop,gpi->ngoi', m_im[:Q], bb_im, precision=hi))
    k = k.at[0].add(d_skip.astype(F32).reshape(G, C)[:, :, None] * jnp.eye(C, dtype=F32)[None])
    eye = jnp.eye(GPB, dtype=F32)

    kt = k.reshape(Q, NGB, GPB, C, C).transpose(0, 1, 2, 4, 3)
    bd = (kt[:, :, :, :, None, :] * eye[None, None, :, None, :, None]).reshape(Q, NGB, LANES, LANES)
    bd = jnp.concatenate([jnp.zeros_like(bd[:1]), bd], axis=0)
    tiles = []
    for dd in range(Q // 2 - 1, -1, -1):
        top = jnp.concatenate([bd[2 * dd + 1], bd[2 * dd + 2]], axis=-1)
        bot = jnp.concatenate([bd[2 * dd], bd[2 * dd + 1]], axis=-1)
        tiles.append(jnp.concatenate([top, bot], axis=-2))
    r_rev = jnp.concatenate(tiles, axis=-2).astype(BF16)

    wr, wi = _cmul(pr[:Q][::-1][:, :, :, None], pi[:Q][::-1][:, :, :, None], bb_re[None], bb_im[None])

    def w_layout(t):
        t = t.reshape(Q, NGB, GPB, P, C).transpose(1, 0, 2, 4, 3)
        t = t[:, :, :, :, None, :] * eye[None, None, :, None, :, None]
        return t.reshape(NGB, Q * LANES, GPB * P)

    w = jnp.concatenate([w_layout(wr), w_layout(wi)], axis=-1).astype(BF16)

    def v_layout(t):
        t = t.reshape(Q, NGB, GPB, C, P).transpose(1, 2, 4, 0, 3)
        t = t[:, :, :, :, None, :] * eye[None, :, None, None, :, None]
        return t.reshape(NGB, GPB * P, Q * LANES)

    v = jnp.concatenate([v_layout(m_re[1:]), v_layout(-m_im[1:])], axis=-2).astype(BF16)

    def slabs(t):
        lead = t.shape[:-2]
        t = t.reshape(lead + (NGB, GPB * P // LANES, LANES))
        return jnp.moveaxis(t, (-3, -2), (0, 1))

    q_re, q_im = apow(float(Q))
    e_re, e_im = apow(float(Q * SEG))
    cst_re = jnp.concatenate([jnp.broadcast_to(slabs(q_re)[:, :, None, :], (NGB, GPB * P // LANES, SUBLANES, LANES)),
                              jnp.broadcast_to(slabs(e_re)[:, :, None, :], (NGB, GPB * P // LANES, SUBLANES, LANES))], axis=2)
    cst_im = jnp.concatenate([jnp.broadcast_to(slabs(q_im)[:, :, None, :], (NGB, GPB * P // LANES, SUBLANES, LANES)),
                              jnp.broadcast_to(slabs(e_im)[:, :, None, :], (NGB, GPB * P // LANES, SUBLANES, LANES))], axis=2)
    cst = jnp.concatenate([cst_re, cst_im], axis=1)
    t_re, t_im = apow(float(Q) * jnp.arange(SEG, dtype=F32)[:, None, None])
    pw = jnp.concatenate([slabs(t_re), slabs(t_im)], axis=1)
    return r_rev, w, v, cst, pw


def _s5_layer(x, g, lam_re, lam_im, log_dt, b_re, b_im, c_re, c_im, d_skip, w_glu, b_glu, *, ma=64, mc=32):
    bsz, L, d = x.shape
    rows = bsz * L // Q
    rows_b = L // Q
    assert rows_b == NSEG * SEG and d == NGB * LANES
    r_rev, w, v, cst, pw = _s5_tables(lam_re, lam_im, log_dt, b_re, b_im, c_re, c_im, d_skip)
    x2 = x.reshape(rows, Q * d)
    cp = pltpu.CompilerParams(dimension_semantics=("arbitrary",), vmem_limit_bytes=VMEM_LIMIT)

    h = pl.pallas_call(
        _s5_norm_kernel,
        grid=(rows // ma,),
        in_specs=[pl.BlockSpec((ma, Q * d), lambda i: (i, 0)), _const_spec((1, d))],
        out_specs=pl.BlockSpec((NGB, ma, Q * LANES), lambda i: (0, i, 0)),
        out_shape=jax.ShapeDtypeStruct((NGB, rows, Q * LANES), BF16),
        compiler_params=cp,
        name="s5_norm",
    )(x2, g.reshape(1, d))

    nstate = w.shape[-1]
    nslab = nstate // LANES
    y = pl.pallas_call(
        _s5_core_kernel,
        grid=(NGB, bsz),
        in_specs=[
            pl.BlockSpec((1, rows_b, Q * LANES), lambda gb, b: (gb, b, 0)),
            pl.BlockSpec((1, Q * LANES, nstate), lambda gb, b: (gb, 0, 0)),
            pl.BlockSpec((1, Q * LANES, MXU_DIM), lambda gb, b: (gb, 0, 0)),
            pl.BlockSpec((1, nstate, Q * LANES), lambda gb, b: (gb, 0, 0)),
            pl.BlockSpec((1, nslab, 2 * SUBLANES, LANES), lambda gb, b: (gb, 0, 0, 0)),
            pl.BlockSpec((1, nslab, SEG, LANES), lambda gb, b: (gb, 0, 0, 0)),
        ],
        out_specs=pl.BlockSpec((1, rows_b, Q * LANES), lambda gb, b: (gb, b, 0)),
        out_shape=jax.ShapeDtypeStruct((NGB, rows, Q * LANES), BF16),
        scratch_shapes=[
            pltpu.VMEM((nslab, NSEG * PITCH, LANES), F32),
            pltpu.VMEM((nslab, NSEG * PITCH, LANES), F32),
        ],
        compiler_params=pltpu.CompilerParams(
            dimension_semantics=("arbitrary", "arbitrary"), vmem_limit_bytes=VMEM_LIMIT),
        name="s5_core",
    )(h, w, r_rev, v, cst, pw)

    out = pl.pallas_call(
        _s5_out_kernel,
        grid=(rows // mc,),
        in_specs=[
            pl.BlockSpec((mc, Q * d), lambda i: (i, 0)),
            pl.BlockSpec((NGB, mc, Q * LANES), lambda i: (0, i, 0)),
            _const_spec((d, d)),
            _const_spec((1, d)),
        ],
        out_specs=pl.BlockSpec((mc, Q * d), lambda i: (i, 0)),
        out_shape=jax.ShapeDtypeStruct((rows, Q * d), F32),
        scratch_shapes=[pltpu.VMEM((Q * mc, d), F32)],
        compiler_params=cp,
        name="s5_out",
    )(x2, y, w_glu.astype(BF16), b_glu.reshape(1, d))
    return out.reshape(bsz, L, d)


def kernel(x, positions, norm_mix, norm_ffn, norm_final, s5_lambda_re, s5_lambda_im, s5_log_dt, s5_b_re, s5_b_im, s5_c_re, s5_c_im, s5_d, s5_w_glu, s5_b_glu, attn_w_qkv, attn_b_qkv, attn_sinks, attn_w_o, attn_b_o, ffn_w_up, ffn_w_conv, ffn_b_conv, ffn_w_down):
    depth = norm_mix.shape[0]
    for i in range(depth):
        j = i // 2
        if i % 2 == 0:
            x = _s5_layer(x, norm_mix[i], s5_lambda_re[j], s5_lambda_im[j], s5_log_dt[j],
                          s5_b_re[j], s5_b_im[j], s5_c_re[j], s5_c_im[j], s5_d[j], s5_w_glu[j], s5_b_glu[j])
        else:
            x = _swa_layer(x, positions, norm_mix[i], attn_w_qkv[j], attn_b_qkv[j], attn_sinks[j],
                           attn_w_o[j], attn_b_o[j])
        x = _conv_ffn(x, norm_ffn[i], ffn_w_up[i], ffn_w_conv[i], ffn_b_conv[i], ffn_w_down[i],
                      norm_final if i == depth - 1 else None)
    return x
```

```python
import functools

import numpy as np
import jax
import jax.numpy as jnp
from jax import lax
from jax.experimental import pallas as pl
from jax.experimental.pallas import tpu as pltpu

F32 = jnp.float32
BF16 = jnp.bfloat16

EPS = 1e-5
NEG_INF = -1e30
GELU_C0 = float(np.sqrt(2.0 / np.pi).astype(np.float32))

HEAD_DIM = 64
N_KV_HEADS = 4
ROPE_DIM = 16
ROPE_THETA = 500000.0
ATTN_BLOCK = 128
S5_GROUP = 16

LANES = 128
SUBLANES = 8
MXU_DIM = 256
VMEM_LIMIT = 56 * 1024 * 1024

Q = 16
NGB = 8
GPB = LANES // S5_GROUP
SEG = 64
NSEG = 8
PITCH = 72


def _rms(x, g):
    ms = jnp.mean(x * x, axis=-1, keepdims=True)
    return x * lax.rsqrt(ms + EPS) * g


def _cmul(ar, ai, br, bi):
    return ar * br - ai * bi, ar * bi + ai * br


def _const_spec(shape):
    nd = len(shape)
    return pl.BlockSpec(shape, lambda *_: (0,) * nd, pipeline_mode=pl.Buffered(1))


def _ffn_kernel(x_ref, g_ref, wu_ref, wc_ref, bc_ref, wd_ref, gf_ref, o_ref, act_ref, carry_ref,
                *, fb, final_norm):
    tm = x_ref.shape[1]
    d_ff = wd_ref.shape[0]

    @pl.when(pl.program_id(1) == 0)
    def _():
        carry_ref[...] = jnp.zeros_like(carry_ref)

    x = x_ref[0]
    h = _rms(x, g_ref[...]).astype(BF16)
    rows = lax.broadcasted_iota(jnp.int32, (SUBLANES, fb), 0)

    def conv_block(col):
        u = jnp.dot(h, wu_ref[:, col:col + fb], preferred_element_type=F32)
        prev = carry_ref[:, col:col + fb]
        carry_ref[:, col:col + fb] = u[tm - SUBLANES:, :]
        s1 = pltpu.roll(u, 1, 0)
        s2 = pltpu.roll(u, 2, 0)
        t1 = jnp.where(rows < 1, pltpu.roll(prev, 1, 0), s1[:SUBLANES])
        t2 = jnp.where(rows < 2, pltpu.roll(prev, 2, 0), s2[:SUBLANES])
        s1 = jnp.concatenate([t1, s1[SUBLANES:]], axis=0)
        s2 = jnp.concatenate([t2, s2[SUBLANES:]], axis=0)
        w = wc_ref[:, col:col + fb]
        return w[0:1] * s2 + w[1:2] * s1 + w[2:3] * u + bc_ref[:, col:col + fb]

    for j in range(d_ff // fb):
        a = conv_block(j * fb)
        v = conv_block(d_ff + j * fb)
        act_ref[:, j * fb:(j + 1) * fb] = (a * (1.0 / (1.0 + jnp.exp(-a))) * v).astype(BF16)

    y = jnp.dot(act_ref[...], wd_ref[...], preferred_element_type=F32) + x
    if final_norm:
        y = _rms(y, gf_ref[...])
    o_ref[0] = y


def _conv_ffn(x, g, w_up, w_conv, b_conv, w_down, g_final, *, tm=512, fb=256):
    bsz, L, d = x.shape
    d_ff = w_down.shape[0]
    final_norm = g_final is not None
    gf = g_final if final_norm else g
    kern = functools.partial(_ffn_kernel, fb=fb, final_norm=final_norm)
    return pl.pallas_call(
        kern,
        grid=(bsz, L // tm),
        in_specs=[
            pl.BlockSpec((1, tm, d), lambda b, i: (b, i, 0)),
            _const_spec((1, d)),
            _const_spec((d, 2 * d_ff)),
            _const_spec((3, 2 * d_ff)),
            _const_spec((1, 2 * d_ff)),
            _const_spec((d_ff, d)),
            _const_spec((1, d)),
        ],
        out_specs=pl.BlockSpec((1, tm, d), lambda b, i: (b, i, 0)),
        out_shape=jax.ShapeDtypeStruct(x.shape, F32),
        scratch_shapes=[
            pltpu.VMEM((tm, d_ff), BF16),
            pltpu.VMEM((SUBLANES, 2 * d_ff), F32),
        ],
        compiler_params=pltpu.CompilerParams(
            dimension_semantics=("arbitrary", "arbitrary"), vmem_limit_bytes=VMEM_LIMIT),
        name="conv_ffn",
    )(x, g.reshape(1, d), w_up.astype(BF16), w_conv, b_conv.reshape(1, -1), w_down.astype(BF16),
      gf.reshape(1, d))


def _qkv_kernel(x_ref, pos_ref, g_ref, w_ref, b_ref, fr_ref, sg_ref, q_ref, kv_ref):
    d = x_ref.shape[2]
    n_rot = (d + N_KV_HEADS * HEAD_DIM) // LANES
    h = _rms(x_ref[0], g_ref[...]).astype(BF16)
    qkv = jnp.dot(h, w_ref[...], preferred_element_type=F32) + b_ref[...]
    ang = pos_ref[0].astype(F32) * fr_ref[...]
    cos = jnp.cos(ang)
    sin = jnp.sin(ang) * sg_ref[...]
    lane = lax.broadcasted_iota(jnp.int32, ang.shape, 1)
    low_half = (lane & (HEAD_DIM - 1)) < (ROPE_DIM // 2)
    for cb in range(n_rot):
        t = qkv[:, cb * LANES:(cb + 1) * LANES]
        partner = jnp.where(low_half, pltpu.roll(t, LANES - ROPE_DIM // 2, 1), pltpu.roll(t, ROPE_DIM // 2, 1))
        o = t * cos + partner * sin
        if cb * LANES < d:
            q_ref[0, :, cb * LANES:(cb + 1) * LANES] = (o * (HEAD_DIM ** -0.5)).astype(BF16)
        else:
            kv_ref[0, :, cb * LANES - d:(cb + 1) * LANES - d] = o.astype(BF16)
    nkv = N_KV_HEADS * HEAD_DIM
    kv_ref[0, :, nkv:] = qkv[:, d + nkv:].astype(BF16)


def _attn_kernel(sink_ref, q_ref, kvc_ref, kvp_ref, x_ref, wo_ref, bo_ref, o_ref, oh_ref):
    tq = q_ref.shape[1]
    blk = ATTN_BLOCK
    nkv = N_KV_HEADS * HEAD_DIM
    q_per_kv = q_ref.shape[2] // nkv
    first_tile = pl.program_id(1) == 0
    row = lax.broadcasted_iota(jnp.int32, (q_per_kv * blk, 2 * blk), 0)
    col = lax.broadcasted_iota(jnp.int32, (q_per_kv * blk, 2 * blk), 1)
    qi = row & (blk - 1)
    valid = jnp.logical_or(jnp.logical_and(col < blk, col > qi), jnp.logical_and(col >= blk, col - blk <= qi))
    valid_first = jnp.logical_and(valid, jnp.logical_or(col >= blk, jnp.logical_not(first_tile)))
    head_of_row = lax.broadcasted_iota(jnp.int32, (q_per_kv * blk, 1), 0) // blk
    sinks = []
    for hk in range(N_KV_HEADS):
        sk = jnp.zeros((q_per_kv * blk, 1), F32)
        for g in range(q_per_kv):
            sk = jnp.where(head_of_row == g, sink_ref[hk * q_per_kv + g], sk)
        sinks.append(sk)
    dn = (((1,), (1,)), ((), ()))
    for n in range(tq // blk):
        rows = slice(n * blk, (n + 1) * blk)
        if n == 0:
            kv2 = jnp.concatenate([kvp_ref[0], kvc_ref[0, rows, :]], axis=0)
            mask = valid_first
        else:
            kv2 = kvc_ref[0, (n - 1) * blk:(n + 1) * blk, :]
            mask = valid
        for hk in range(N_KV_HEADS):
            k2 = kv2[:, hk * HEAD_DIM:(hk + 1) * HEAD_DIM]
            v2 = kv2[:, nkv + hk * HEAD_DIM:nkv + (hk + 1) * HEAD_DIM]
            q4 = jnp.concatenate(
                [q_ref[0, rows, (hk * q_per_kv + g) * HEAD_DIM:(hk * q_per_kv + g + 1) * HEAD_DIM]
                 for g in range(q_per_kv)], axis=0)
            s = jnp.where(mask, lax.dot_general(q4, k2, dn, preferred_element_type=F32), NEG_INF)
            m = jnp.maximum(jnp.max(s, axis=-1, keepdims=True), sinks[hk])
            p = jnp.exp(s - m)
            den = jnp.sum(p, axis=-1, keepdims=True) + jnp.exp(sinks[hk] - m)
            o = jnp.dot(p.astype(BF16), v2, preferred_element_type=F32) / den
            for g in range(q_per_kv):
                hq = hk * q_per_kv + g
                oh_ref[rows, hq * HEAD_DIM:(hq + 1) * HEAD_DIM] = o[g * blk:(g + 1) * blk].astype(BF16)
    o_ref[0] = jnp.dot(oh_ref[...], wo_ref[...], preferred_element_type=F32) + bo_ref[...] + x_ref[0]


def _swa_layer(x, pos, g, w_qkv, b_qkv, sinks, w_o, b_o, *, tm=512, tq=512):
    bsz, L, d = x.shape
    nkv = N_KV_HEADS * HEAD_DIM
    qkv_dim = w_qkv.shape[1]
    half = ROPE_DIM // 2
    inv_freq = 1.0 / jnp.power(ROPE_THETA, jnp.arange(0, ROPE_DIM, 2, dtype=F32) / ROPE_DIM)
    dim = jnp.arange(LANES) % HEAD_DIM
    freq = jnp.where(dim < ROPE_DIM, inv_freq[dim % half], 0.0).reshape(1, LANES)
    sign = jnp.where(dim < half, -1.0, 1.0).astype(F32).reshape(1, LANES)

    q, kv = pl.pallas_call(
        _qkv_kernel,
        grid=(bsz, L // tm),
        in_specs=[
            pl.BlockSpec((1, tm, d), lambda b, i: (b, i, 0)),
            pl.BlockSpec((1, tm, 1), lambda b, i: (b, i, 0)),
            _const_spec((1, d)),
            _const_spec((d, qkv_dim)),
            _const_spec((1, qkv_dim)),
            _const_spec((1, LANES)),
            _const_spec((1, LANES)),
        ],
        out_specs=[
            pl.BlockSpec((1, tm, d), lambda b, i: (b, i, 0)),
            pl.BlockSpec((1, tm, 2 * nkv), lambda b, i: (b, i, 0)),
        ],
        out_shape=[
            jax.ShapeDtypeStruct((bsz, L, d), BF16),
            jax.ShapeDtypeStruct((bsz, L, 2 * nkv), BF16),
        ],
        compiler_params=pltpu.CompilerParams(
            dimension_semantics=("arbitrary", "arbitrary"), vmem_limit_bytes=VMEM_LIMIT),
        name="qkv_rope",
    )(x, pos.reshape(bsz, L, 1), g.reshape(1, d), w_qkv.astype(BF16), b_qkv.reshape(1, -1), freq, sign)

    bpt = tq // ATTN_BLOCK
    return pl.pallas_call(
        _attn_kernel,
        grid=(bsz, L // tq),
        in_specs=[
            pl.BlockSpec(memory_space=pltpu.SMEM),
            pl.BlockSpec((1, tq, d), lambda b, i: (b, i, 0)),
            pl.BlockSpec((1, tq, 2 * nkv), lambda b, i: (b, i, 0)),
            pl.BlockSpec((1, ATTN_BLOCK, 2 * nkv), lambda b, i: (b, jnp.maximum(i * bpt - 1, 0), 0)),
            pl.BlockSpec((1, tq, d), lambda b, i: (b, i, 0)),
            _const_spec((d, d)),
            _const_spec((1, d)),
        ],
        out_specs=pl.BlockSpec((1, tq, d), lambda b, i: (b, i, 0)),
        out_shape=jax.ShapeDtypeStruct(x.shape, F32),
        scratch_shapes=[pltpu.VMEM((tq, d), BF16)],
        compiler_params=pltpu.CompilerParams(
            dimension_semantics=("arbitrary", "arbitrary"), vmem_limit_bytes=VMEM_LIMIT),
        name="swa_attn",
    )(sinks.astype(F32), q, kv, kv, x, w_o.astype(BF16), b_o.reshape(1, d))


def _s5_norm_kernel(x_ref, g_ref, h_ref, hs_ref):
    ma = h_ref.shape[1]
    hs = _rms(x_ref[...], g_ref[...])
    for gb in range(NGB):
        hs_ref[gb] = hs[:, gb * LANES:(gb + 1) * LANES]
    for s in range(Q):
        for gb in range(NGB):
            h_ref[gb, :, s * LANES:(s + 1) * LANES] = hs_ref[gb, pl.ds(s, ma, stride=Q), :].astype(BF16)


def _s5_core_kernel(h_ref, w_ref, r_ref, vt_ref, pw_ref, y_ref, z_ref, sx_ref):
    nslab = z_ref.shape[0]
    nc = nslab // 2
    lhs = h_ref[0]
    z = jnp.dot(lhs, w_ref[0], preferred_element_type=F32)
    for j in range(NSEG):
        for c in range(nslab):
            z_ref[c, j * PITCH:j * PITCH + SEG, :] = z[j * SEG:(j + 1) * SEG, c * LANES:(c + 1) * LANES]

    half = nc * LANES

    def table(row0, nrows, c, imag):
        lo = (half if imag else 0) + c * LANES
        return pw_ref[0, row0:row0 + nrows, lo:lo + LANES]

    a_re = [jnp.broadcast_to(table(1, 1, c, False), (NSEG, LANES)) for c in range(nc)]
    a_im = [jnp.broadcast_to(table(1, 1, c, True), (NSEG, LANES)) for c in range(nc)]

    def scan_step(t, state):
        new_re, new_im = [], []
        for c in range(nc):
            s_re, s_im = state[c], state[c + nc]
            sx_ref[c, pl.ds(t, NSEG, stride=PITCH), :] = s_re
            sx_ref[c + nc, pl.ds(t, NSEG, stride=PITCH), :] = s_im
            z_re = z_ref[c, pl.ds(t, NSEG, stride=PITCH), :]
            z_im = z_ref[c + nc, pl.ds(t, NSEG, stride=PITCH), :]
            new_re.append(a_re[c] * s_re - a_im[c] * s_im + z_re)
            new_im.append(a_re[c] * s_im + a_im[c] * s_re + z_im)
        return tuple(new_re + new_im)

    zero = jnp.zeros((NSEG, LANES), F32)
    end = lax.fori_loop(0, SEG, scan_step, (zero,) * nslab)

    for c in range(nc):
        ap_re = table(SEG, 1, c, False)
        ap_im = table(SEG, 1, c, True)
        c_re = jnp.zeros((1, LANES), F32)
        c_im = jnp.zeros((1, LANES), F32)
        for j in range(1, NSEG):
            e_re = end[c][j - 1:j]
            e_im = end[c + nc][j - 1:j]
            c_re, c_im = (e_re + ap_re * c_re - ap_im * c_im, e_im + ap_re * c_im + ap_im * c_re)
            p_re = table(0, SEG, c, False)
            p_im = table(0, SEG, c, True)
            rows = slice(j * PITCH, j * PITCH + SEG)
            sx_ref[c, rows, :] = sx_ref[c, rows, :] + (p_re * c_re - p_im * c_im)
            sx_ref[c + nc, rows, :] = sx_ref[c + nc, rows, :] + (p_re * c_im + p_im * c_re)

    sx = jnp.concatenate(
        [jnp.concatenate([sx_ref[c, j * PITCH:j * PITCH + SEG, :] for c in range(nslab)], axis=1)
         for j in range(NSEG)], axis=0).astype(BF16)

    nt = lhs.shape[1] // MXU_DIM
    for b in range(nt):
        acc = jnp.dot(lhs[:, :(b + 1) * MXU_DIM], r_ref[0, (nt - 1 - b) * MXU_DIM:, :],
                      preferred_element_type=F32)
        acc = acc + lax.dot_general(sx, vt_ref[0, b * MXU_DIM:(b + 1) * MXU_DIM, :], (((1,), (1,)), ((), ())),
                                    preferred_element_type=F32)
        y_ref[0, :, b * MXU_DIM:(b + 1) * MXU_DIM] = acc.astype(BF16)


def _s5_out_kernel(x_ref, y_ref, w_ref, b_ref, o_ref, g_ref):
    mc = y_ref.shape[1]
    c0 = GELU_C0
    for s in range(Q):
        for gb in range(NGB):
            y = y_ref[gb, :, s * LANES:(s + 1) * LANES].astype(F32)
            g_ref[gb, pl.ds(s, mc, stride=Q), :] = 0.5 * y * (1.0 + jnp.tanh(c0 * (y + 0.044715 * (y * y * y))))
    g = jnp.concatenate([g_ref[gb] for gb in range(NGB)], axis=1)
    gate = jnp.dot(g.astype(BF16), w_ref[...], preferred_element_type=F32) + b_ref[...]
    o_ref[...] = x_ref[...] + g * (1.0 / (1.0 + jnp.exp(-gate)))


def _s5_prep_kernel(lr_ref, li_ref, ldt_ref, be_re_ref, be_im_ref, ce_re_ref, ce_im_ref, d_ref,
                    w_ref, vt_ref, r_ref, pw_ref, wf_ref):
    nst = lr_ref.shape[2]
    lr, li = lr_ref[0], li_ref[0]
    dt = jnp.exp(ldt_ref[0])
    th_re, th_im = lr * dt, li * dt

    def apow(n):
        mag = jnp.exp(n * th_re)
        return mag * jnp.cos(n * th_im), mag * jnp.sin(n * th_im)

    n_small = lax.broadcasted_iota(jnp.int32, (3 * SUBLANES, 1), 0).astype(F32)
    p_re, p_im = apow(n_small)
    a_re, a_im = p_re[1:2], p_im[1:2]
    den = lr * lr + li * li
    z_re = ((a_re - 1.0) * lr + a_im * li) / den
    z_im = (a_im * lr - (a_re - 1.0) * li) / den
    bb_re, bb_im = _cmul(z_re, z_im, be_re_ref[0], be_im_ref[0])
    ce_re, ce_im = ce_re_ref[0], ce_im_ref[0]

    for s in range(Q):
        rows = slice(s * LANES, (s + 1) * LANES)
        n = Q - 1 - s
        w_re, w_im = _cmul(bb_re, bb_im, p_re[n:n + 1], p_im[n:n + 1])
        wf_ref[rows, :nst] = w_re
        wf_ref[rows, nst:] = w_im
        w_ref[0, rows, :nst] = w_re.astype(BF16)
        w_ref[0, rows, nst:] = w_im.astype(BF16)
        v_re, v_im = _cmul(ce_re, ce_im, p_re[s + 1:s + 2], p_im[s + 1:s + 2])
        vt_ref[0, rows, :nst] = v_re.astype(BF16)
        vt_ref[0, rows, nst:] = (-v_im).astype(BF16)

    dn = (((1,), (1,)), ((), ()))
    hi = lax.Precision.HIGHEST
    kall = (lax.dot_general(wf_ref[:, :nst], ce_re, dn, precision=hi, preferred_element_type=F32)
            - lax.dot_general(wf_ref[:, nst:], ce_im, dn, precision=hi, preferred_element_type=F32))
    ri = lax.broadcasted_iota(jnp.int32, (LANES, LANES), 0)
    ci = lax.broadcasted_iota(jnp.int32, (LANES, LANES), 1)
    lag0 = kall[(Q - 1) * LANES:] + jnp.where(ri == ci, d_ref[0], 0.0)
    kall = jnp.concatenate([kall[:(Q - 1) * LANES], lag0], axis=0)
    r_ref[0, :, LANES:] = kall.astype(BF16)
    r_ref[0, :, :LANES] = jnp.concatenate([kall[LANES:], jnp.zeros((LANES, LANES), F32)], axis=0).astype(BF16)

    n_big = float(Q) * lax.broadcasted_iota(jnp.int32, (pw_ref.shape[1], 1), 0).astype(F32)
    t_re, t_im = apow(n_big)
    pw_ref[0, :, :nst] = t_re
    pw_ref[0, :, nst:] = t_im


def _s5_tables(lam_re, lam_im, log_dt, b_re, b_im, c_re, c_im, d_skip):
    G, P = lam_re.shape
    C = b_re.shape[-1]
    nst = GPB * P
    eye = jnp.eye(GPB, dtype=bool)

    def expand(t):
        t = t.astype(F32).reshape(NGB, GPB, C, P)
        t = jnp.where(eye[None, :, None, :, None], t[:, :, :, None, :], 0.0)
        return t.reshape(NGB, GPB * C, nst)

    def lanes(t):
        return t.astype(F32).reshape(NGB, 1, nst)

    args = (lanes(lam_re), lanes(lam_im), lanes(jnp.broadcast_to(log_dt[:, None], (G, P))),
            expand(jnp.swapaxes(b_re, 1, 2)), expand(jnp.swapaxes(b_im, 1, 2)), expand(c_re), expand(c_im),
            d_skip.astype(F32).reshape(NGB, 1, LANES))
    row_spec = pl.BlockSpec((1, 1, nst), lambda gb: (gb, 0, 0))
    mat_spec = pl.BlockSpec((1, LANES, nst), lambda gb: (gb, 0, 0))
    n_pw = SEG + SUBLANES
    return pl.pallas_call(
        _s5_prep_kernel,
        grid=(NGB,),
        in_specs=[row_spec, row_spec, row_spec, mat_spec, mat_spec, mat_spec, mat_spec,
                  pl.BlockSpec((1, 1, LANES), lambda gb: (gb, 0, 0))],
        out_specs=[
            pl.BlockSpec((1, Q * LANES, 2 * nst), lambda gb: (gb, 0, 0)),
            pl.BlockSpec((1, Q * LANES, 2 * nst), lambda gb: (gb, 0, 0)),
            pl.BlockSpec((1, Q * LANES, MXU_DIM), lambda gb: (gb, 0, 0)),
            pl.BlockSpec((1, n_pw, 2 * nst), lambda gb: (gb, 0, 0)),
        ],
        out_shape=[
            jax.ShapeDtypeStruct((NGB, Q * LANES, 2 * nst), BF16),
            jax.ShapeDtypeStruct((NGB, Q * LANES, 2 * nst), BF16),
            jax.ShapeDtypeStruct((NGB, Q * LANES, MXU_DIM), BF16),
            jax.ShapeDtypeStruct((NGB, n_pw, 2 * nst), F32),
        ],
        scratch_shapes=[pltpu.VMEM((Q * LANES, 2 * nst), F32)],
        compiler_params=pltpu.CompilerParams(dimension_semantics=("arbitrary",), vmem_limit_bytes=VMEM_LIMIT),
        name="s5_prep",
    )(*args)


def _s5_layer(x, g, lam_re, lam_im, log_dt, b_re, b_im, c_re, c_im, d_skip, w_glu, b_glu, *, ma=64, mc=32):
    bsz, L, d = x.shape
    rows = bsz * L // Q
    rows_b = L // Q
    assert rows_b == NSEG * SEG and d == NGB * LANES
    w, vt, r_rev, pw = _s5_tables(lam_re, lam_im, log_dt, b_re, b_im, c_re, c_im, d_skip)
    xt = x.reshape(bsz * L, d)
    cp = pltpu.CompilerParams(dimension_semantics=("arbitrary",), vmem_limit_bytes=VMEM_LIMIT)

    h = pl.pallas_call(
        _s5_norm_kernel,
        grid=(rows // ma,),
        in_specs=[pl.BlockSpec((ma * Q, d), lambda i: (i, 0)), _const_spec((1, d))],
        out_specs=pl.BlockSpec((NGB, ma, Q * LANES), lambda i: (0, i, 0)),
        out_shape=jax.ShapeDtypeStruct((NGB, rows, Q * LANES), BF16),
        scratch_shapes=[pltpu.VMEM((NGB, ma * Q, LANES), F32)],
        compiler_params=cp,
        name="s5_norm",
    )(xt, g.reshape(1, d))

    nstate = w.shape[-1]
    nslab = nstate // LANES
    y = pl.pallas_call(
        _s5_core_kernel,
        grid=(NGB, bsz),
        in_specs=[
            pl.BlockSpec((1, rows_b, Q * LANES), lambda gb, b: (gb, b, 0)),
            pl.BlockSpec((1, Q * LANES, nstate), lambda gb, b: (gb, 0, 0)),
            pl.BlockSpec((1, Q * LANES, MXU_DIM), lambda gb, b: (gb, 0, 0)),
            pl.BlockSpec((1, Q * LANES, nstate), lambda gb, b: (gb, 0, 0)),
            pl.BlockSpec((1, pw.shape[1], nstate), lambda gb, b: (gb, 0, 0)),
        ],
        out_specs=pl.BlockSpec((1, rows_b, Q * LANES), lambda gb, b: (gb, b, 0)),
        out_shape=jax.ShapeDtypeStruct((NGB, rows, Q * LANES), BF16),
        scratch_shapes=[
            pltpu.VMEM((nslab, NSEG * PITCH, LANES), F32),
            pltpu.VMEM((nslab, NSEG * PITCH, LANES), F32),
        ],
        compiler_params=pltpu.CompilerParams(
            dimension_semantics=("arbitrary", "arbitrary"), vmem_limit_bytes=VMEM_LIMIT),
        name="s5_core",
    )(h, w, r_rev, vt, pw)

    out = pl.pallas_call(
        _s5_out_kernel,
        grid=(rows // mc,),
        in_specs=[
            pl.BlockSpec((mc * Q, d), lambda i: (i, 0)),
            pl.BlockSpec((NGB, mc, Q * LANES), lambda i: (0, i, 0)),
            _const_spec((d, d)),
            _const_spec((1, d)),
        ],
        out_specs=pl.BlockSpec((mc * Q, d), lambda i: (i, 0)),
        out_shape=jax.ShapeDtypeStruct((bsz * L, d), F32),
        scratch_shapes=[pltpu.VMEM((NGB, mc * Q, LANES), F32)],
        compiler_params=cp,
        name="s5_out",
    )(xt, y, w_glu.astype(BF16), b_glu.reshape(1, d))
    return out.reshape(bsz, L, d)


def kernel(x, positions, norm_mix, norm_ffn, norm_final, s5_lambda_re, s5_lambda_im, s5_log_dt, s5_b_re, s5_b_im, s5_c_re, s5_c_im, s5_d, s5_w_glu, s5_b_glu, attn_w_qkv, attn_b_qkv, attn_sinks, attn_w_o, attn_b_o, ffn_w_up, ffn_w_conv, ffn_b_conv, ffn_w_down):
    depth = norm_mix.shape[0]
    for i in range(depth):
        j = i // 2
        if i % 2 == 0:
            x = _s5_layer(x, norm_mix[i], s5_lambda_re[j], s5_lambda_im[j], s5_log_dt[j],
                          s5_b_re[j], s5_b_im[j], s5_c_re[j], s5_c_im[j], s5_d[j], s5_w_glu[j], s5_b_glu[j])
        else:
            x = _swa_layer(x, positions, norm_mix[i], attn_w_qkv[j], attn_b_qkv[j], attn_sinks[j],
                           attn_w_o[j], attn_b_o[j])
        x = _conv_ffn(x, norm_ffn[i], ffn_w_up[i], ffn_w_conv[i], ffn_b_conv[i], ffn_w_down[i],
                      norm_final if i == depth - 1 else None)
    return x
```

```python
import functools

import numpy as np
import jax
import jax.numpy as jnp
from jax import lax
from jax.experimental import pallas as pl
from jax.experimental.pallas import tpu as pltpu

F32 = jnp.float32
BF16 = jnp.bfloat16

EPS = 1e-5
NEG_INF = -1e30
GELU_C0 = float(np.sqrt(2.0 / np.pi).astype(np.float32))

HEAD_DIM = 64
N_KV_HEADS = 4
ROPE_DIM = 16
ROPE_THETA = 500000.0
ATTN_BLOCK = 128
S5_GROUP = 16

LANES = 128
SUBLANES = 8
MXU_DIM = 256
VMEM_LIMIT = 56 * 1024 * 1024

Q = 16
NGB = 8
GPB = LANES // S5_GROUP
SEG = 64
NSEG = 8
PITCH = 72


def _rms(x, g):
    ms = jnp.mean(x * x, axis=-1, keepdims=True)
    return x * lax.rsqrt(ms + EPS) * g


def _cmul(ar, ai, br, bi):
    return ar * br - ai * bi, ar * bi + ai * br


def _const_spec(shape):
    nd = len(shape)
    return pl.BlockSpec(shape, lambda *_: (0,) * nd, pipeline_mode=pl.Buffered(1))


def _ffn_kernel(x_ref, g_ref, wu_ref, wc_ref, bc_ref, wd_ref, gf_ref, o_ref, act_ref, carry_ref,
                *, fb, final_norm):
    tm = x_ref.shape[1]
    d_ff = wd_ref.shape[0]

    @pl.when(pl.program_id(1) == 0)
    def _():
        carry_ref[...] = jnp.zeros_like(carry_ref)

    x = x_ref[0]
    h = _rms(x, g_ref[...]).astype(BF16)
    rows = lax.broadcasted_iota(jnp.int32, (SUBLANES, fb), 0)

    def conv_block(col):
        u = jnp.dot(h, wu_ref[:, col:col + fb], preferred_element_type=F32)
        prev = carry_ref[:, col:col + fb]
        carry_ref[:, col:col + fb] = u[tm - SUBLANES:, :]
        s1 = pltpu.roll(u, 1, 0)
        s2 = pltpu.roll(u, 2, 0)
        t1 = jnp.where(rows < 1, pltpu.roll(prev, 1, 0), s1[:SUBLANES])
        t2 = jnp.where(rows < 2, pltpu.roll(prev, 2, 0), s2[:SUBLANES])
        s1 = jnp.concatenate([t1, s1[SUBLANES:]], axis=0)
        s2 = jnp.concatenate([t2, s2[SUBLANES:]], axis=0)
        w = wc_ref[:, col:col + fb]
        return w[0:1] * s2 + w[1:2] * s1 + w[2:3] * u + bc_ref[:, col:col + fb]

    for j in range(d_ff // fb):
        a = conv_block(j * fb)
        v = conv_block(d_ff + j * fb)
        act_ref[:, j * fb:(j + 1) * fb] = (a * (1.0 / (1.0 + jnp.exp(-a))) * v).astype(BF16)

    y = jnp.dot(act_ref[...], wd_ref[...], preferred_element_type=F32) + x
    if final_norm:
        y = _rms(y, gf_ref[...])
    o_ref[0] = y


def _conv_ffn(x, g, w_up, w_conv, b_conv, w_down, g_final, *, tm=1024, fb=256):
    bsz, L, d = x.shape
    d_ff = w_down.shape[0]
    final_norm = g_final is not None
    gf = g_final if final_norm else g
    kern = functools.partial(_ffn_kernel, fb=fb, final_norm=final_norm)
    return pl.pallas_call(
        kern,
        grid=(bsz, L // tm),
        in_specs=[
            pl.BlockSpec((1, tm, d), lambda b, i: (b, i, 0)),
            _const_spec((1, d)),
            _const_spec((d, 2 * d_ff)),
            _const_spec((3, 2 * d_ff)),
            _const_spec((1, 2 * d_ff)),
            _const_spec((d_ff, d)),
            _const_spec((1, d)),
        ],
        out_specs=pl.BlockSpec((1, tm, d), lambda b, i: (b, i, 0)),
        out_shape=jax.ShapeDtypeStruct(x.shape, F32),
        scratch_shapes=[
            pltpu.VMEM((tm, d_ff), BF16),
            pltpu.VMEM((SUBLANES, 2 * d_ff), F32),
        ],
        compiler_params=pltpu.CompilerParams(
            dimension_semantics=("arbitrary", "arbitrary"), vmem_limit_bytes=VMEM_LIMIT),
        name="conv_ffn",
    )(x, g.reshape(1, d), w_up.astype(BF16), w_conv, b_conv.reshape(1, -1), w_down.astype(BF16),
      gf.reshape(1, d))


def _qkv_kernel(x_ref, pos_ref, g_ref, wt_ref, bt_ref, fr_ref, qt_ref, k_ref, vt_ref):
    d = x_ref.shape[2]
    nkv = N_KV_HEADS * HEAD_DIM
    half = ROPE_DIM // 2
    h = _rms(x_ref[0], g_ref[...]).astype(BF16)
    qkvt = lax.dot_general(wt_ref[...], h, (((1,), (1,)), ((), ())), preferred_element_type=F32) + bt_ref[...]
    ang = fr_ref[...] * pos_ref[0].astype(F32)
    cos = jnp.cos(ang)
    sin = jnp.sin(ang)

    def rotate(base):
        t1 = qkvt[base:base + half]
        t2 = qkvt[base + half:base + ROPE_DIM]
        return jnp.concatenate([t1 * cos - t2 * sin, t2 * cos + t1 * sin, qkvt[base + ROPE_DIM:base + HEAD_DIM]],
                               axis=0)

    for hq in range(d // HEAD_DIM):
        qt_ref[0, hq * HEAD_DIM:(hq + 1) * HEAD_DIM, :] = rotate(hq * HEAD_DIM).astype(BF16)
    kt = jnp.concatenate([rotate(d + hk * HEAD_DIM) for hk in range(N_KV_HEADS)], axis=0)
    k_ref[0] = kt.T.astype(BF16)
    vt_ref[0] = qkvt[d + nkv:].astype(BF16)


def _attn_kernel(sink_ref, qt_ref, kc_ref, kp_ref, vtc_ref, vtp_ref, x_ref, wo_ref, bo_ref, o_ref, ot_ref):
    tq = x_ref.shape[1]
    blk = ATTN_BLOCK
    q_per_kv = qt_ref.shape[1] // (N_KV_HEADS * HEAD_DIM)
    first_tile = pl.program_id(1) == 0
    key = lax.broadcasted_iota(jnp.int32, (2 * blk, q_per_kv * blk), 0)
    col = lax.broadcasted_iota(jnp.int32, (2 * blk, q_per_kv * blk), 1)
    qi = col & (blk - 1)
    valid = jnp.logical_or(jnp.logical_and(key < blk, key > qi), jnp.logical_and(key >= blk, key - blk <= qi))
    valid_first = jnp.logical_and(valid, jnp.logical_or(key >= blk, jnp.logical_not(first_tile)))
    fills = []
    for hk in range(N_KV_HEADS):
        sk = jnp.full(key.shape, NEG_INF, F32)
        for g in range(q_per_kv):
            in_head = jnp.logical_and(key == 0, jnp.logical_and(col >= g * blk, col < (g + 1) * blk))
            sk = jnp.where(in_head, sink_ref[hk * q_per_kv + g], sk)
        fills.append(sk)
    key_lane = lax.broadcasted_iota(jnp.int32, (HEAD_DIM, 2 * blk), 1)
    for n in range(tq // blk):
        cols = slice(n * blk, (n + 1) * blk)
        if n == 0:
            k2 = jnp.concatenate([kp_ref[0], kc_ref[0, cols, :]], axis=0)
            v2t = jnp.concatenate([vtp_ref[0], vtc_ref[0, :, cols]], axis=1)
            mask = valid_first
        else:
            k2 = kc_ref[0, (n - 1) * blk:(n + 1) * blk, :]
            v2t = vtc_ref[0, :, (n - 1) * blk:(n + 1) * blk]
            mask = valid
        for hk in range(N_KV_HEADS):
            heads = [hk * q_per_kv + g for g in range(q_per_kv)]
            kk = k2[:, hk * HEAD_DIM:(hk + 1) * HEAD_DIM]
            vv = v2t[hk * HEAD_DIM:(hk + 1) * HEAD_DIM, :]
            vv = jnp.where(key_lane == 0, jnp.zeros_like(vv), vv)
            qc = jnp.concatenate([qt_ref[0, hq * HEAD_DIM:(hq + 1) * HEAD_DIM, cols] for hq in heads], axis=1)
            s = jnp.where(mask, jnp.dot(kk, qc, preferred_element_type=F32), fills[hk])
            p = jnp.exp(s - jnp.max(s, axis=0, keepdims=True))
            inv = 1.0 / jnp.sum(p, axis=0, keepdims=True)
            ot = jnp.dot(vv, p.astype(BF16), preferred_element_type=F32) * inv
            for g, hq in enumerate(heads):
                ot_ref[hq * HEAD_DIM:(hq + 1) * HEAD_DIM, cols] = ot[:, g * blk:(g + 1) * blk].astype(BF16)
    attn = lax.dot_general(ot_ref[...], wo_ref[...], (((0,), (0,)), ((), ())), preferred_element_type=F32)
    o_ref[0] = attn + bo_ref[...] + x_ref[0]


def _swa_layer(x, pos, g, w_qkv, b_qkv, sinks, w_o, b_o, *, tm=512, tq=512):
    bsz, L, d = x.shape
    nkv = N_KV_HEADS * HEAD_DIM
    qkv_dim = w_qkv.shape[1]
    inv_freq = 1.0 / jnp.power(ROPE_THETA, jnp.arange(0, ROPE_DIM, 2, dtype=F32) / ROPE_DIM)
    scale = jnp.where(jnp.arange(qkv_dim) < d, HEAD_DIM ** -0.5, 1.0).astype(F32)
    wt = (w_qkv * scale[None, :]).T.astype(BF16)
    bt = (b_qkv * scale).astype(F32).reshape(qkv_dim, 1)

    qt, k, vt = pl.pallas_call(
        _qkv_kernel,
        grid=(bsz, L // tm),
        in_specs=[
            pl.BlockSpec((1, tm, d), lambda b, i: (b, i, 0)),
            pl.BlockSpec((1, 1, tm), lambda b, i: (b, 0, i)),
            _const_spec((1, d)),
            _const_spec((qkv_dim, d)),
            _const_spec((qkv_dim, 1)),
            _const_spec((ROPE_DIM // 2, 1)),
        ],
        out_specs=[
            pl.BlockSpec((1, d, tm), lambda b, i: (b, 0, i)),
            pl.BlockSpec((1, tm, nkv), lambda b, i: (b, i, 0)),
            pl.BlockSpec((1, nkv, tm), lambda b, i: (b, 0, i)),
        ],
        out_shape=[
            jax.ShapeDtypeStruct((bsz, d, L), BF16),
            jax.ShapeDtypeStruct((bsz, L, nkv), BF16),
            jax.ShapeDtypeStruct((bsz, nkv, L), BF16),
        ],
        compiler_params=pltpu.CompilerParams(
            dimension_semantics=("arbitrary", "arbitrary"), vmem_limit_bytes=VMEM_LIMIT),
        name="qkv_rope",
    )(x, pos.reshape(bsz, 1, L), g.reshape(1, d), wt, bt, inv_freq.reshape(-1, 1))

    bpt = tq // ATTN_BLOCK
    prev_block = lambda i: jnp.maximum(i * bpt - 1, 0)
    return pl.pallas_call(
        _attn_kernel,
        grid=(bsz, L // tq),
        in_specs=[
            pl.BlockSpec(memory_space=pltpu.SMEM),
            pl.BlockSpec((1, d, tq), lambda b, i: (b, 0, i)),
            pl.BlockSpec((1, tq, nkv), lambda b, i: (b, i, 0)),
            pl.BlockSpec((1, ATTN_BLOCK, nkv), lambda b, i: (b, prev_block(i), 0)),
            pl.BlockSpec((1, nkv, tq), lambda b, i: (b, 0, i)),
            pl.BlockSpec((1, nkv, ATTN_BLOCK), lambda b, i: (b, 0, prev_block(i))),
            pl.BlockSpec((1, tq, d), lambda b, i: (b, i, 0)),
            _const_spec((d, d)),
            _const_spec((1, d)),
        ],
        out_specs=pl.BlockSpec((1, tq, d), lambda b, i: (b, i, 0)),
        out_shape=jax.ShapeDtypeStruct(x.shape, F32),
        scratch_shapes=[pltpu.VMEM((d, tq), BF16)],
        compiler_params=pltpu.CompilerParams(
            dimension_semantics=("arbitrary", "arbitrary"), vmem_limit_bytes=VMEM_LIMIT),
        name="swa_attn",
    )(sinks.astype(F32), qt, k, k, vt, vt, x, w_o.astype(BF16), b_o.reshape(1, d))


def _s5_norm_kernel(x_ref, g_ref, h_ref, hs_ref):
    ma = h_ref.shape[1]
    hs = _rms(x_ref[...], g_ref[...])
    for gb in range(NGB):
        hs_ref[gb] = hs[:, gb * LANES:(gb + 1) * LANES]
    for s in range(Q):
        for gb in range(NGB):
            h_ref[gb, :, s * LANES:(s + 1) * LANES] = hs_ref[gb, pl.ds(s, ma, stride=Q), :].astype(BF16)


def _s5_core_kernel(h_ref, w_ref, r_ref, vt_ref, pw_ref, y_ref, z_ref, sx_ref):
    nslab = z_ref.shape[0]
    nc = nslab // 2
    lhs = h_ref[0]
    z = jnp.dot(lhs, w_ref[0], preferred_element_type=F32)
    for j in range(NSEG):
        for c in range(nslab):
            z_ref[c, j * PITCH:j * PITCH + SEG, :] = z[j * SEG:(j + 1) * SEG, c * LANES:(c + 1) * LANES]

    half = nc * LANES

    def table(row0, nrows, c, imag):
        lo = (half if imag else 0) + c * LANES
        return pw_ref[0, row0:row0 + nrows, lo:lo + LANES]

    a_re = [jnp.broadcast_to(table(1, 1, c, False), (NSEG, LANES)) for c in range(nc)]
    a_im = [jnp.broadcast_to(table(1, 1, c, True), (NSEG, LANES)) for c in range(nc)]

    def scan_step(t, state):
        new_re, new_im = [], []
        for c in range(nc):
            s_re, s_im = state[c], state[c + nc]
            sx_ref[c, pl.ds(t, NSEG, stride=PITCH), :] = s_re
            sx_ref[c + nc, pl.ds(t, NSEG, stride=PITCH), :] = s_im
            z_re = z_ref[c, pl.ds(t, NSEG, stride=PITCH), :]
            z_im = z_ref[c + nc, pl.ds(t, NSEG, stride=PITCH), :]
            new_re.append(a_re[c] * s_re - a_im[c] * s_im + z_re)
            new_im.append(a_re[c] * s_im + a_im[c] * s_re + z_im)
        return tuple(new_re + new_im)

    zero = jnp.zeros((NSEG, LANES), F32)
    end = lax.fori_loop(0, SEG, scan_step, (zero,) * nslab)

    for c in range(nc):
        ap_re = table(SEG, 1, c, False)
        ap_im = table(SEG, 1, c, True)
        c_re = jnp.zeros((1, LANES), F32)
        c_im = jnp.zeros((1, LANES), F32)
        for j in range(1, NSEG):
            e_re = end[c][j - 1:j]
            e_im = end[c + nc][j - 1:j]
            c_re, c_im = (e_re + ap_re * c_re - ap_im * c_im, e_im + ap_re * c_im + ap_im * c_re)
            p_re = table(0, SEG, c, False)
            p_im = table(0, SEG, c, True)
            rows = slice(j * PITCH, j * PITCH + SEG)
            sx_ref[c, rows, :] = sx_ref[c, rows, :] + (p_re * c_re - p_im * c_im)
            sx_ref[c + nc, rows, :] = sx_ref[c + nc, rows, :] + (p_re * c_im + p_im * c_re)

    sx = jnp.concatenate(
        [jnp.concatenate([sx_ref[c, j * PITCH:j * PITCH + SEG, :] for c in range(nslab)], axis=1)
         for j in range(NSEG)], axis=0).astype(BF16)

    nt = lhs.shape[1] // MXU_DIM
    for b in range(nt):
        acc = jnp.dot(lhs[:, :(b + 1) * MXU_DIM], r_ref[0, (nt - 1 - b) * MXU_DIM:, :],
                      preferred_element_type=F32)
        acc = acc + lax.dot_general(sx, vt_ref[0, b * MXU_DIM:(b + 1) * MXU_DIM, :], (((1,), (1,)), ((), ())),
                                    preferred_element_type=F32)
        y_ref[0, :, b * MXU_DIM:(b + 1) * MXU_DIM] = acc.astype(BF16)


def _s5_out_kernel(x_ref, y_ref, w_ref, b_ref, o_ref, g_ref):
    mc = y_ref.shape[1]
    c0 = GELU_C0
    for s in range(Q):
        for gb in range(NGB):
            y = y_ref[gb, :, s * LANES:(s + 1) * LANES].astype(F32)
            g_ref[gb, pl.ds(s, mc, stride=Q), :] = 0.5 * y * (1.0 + jnp.tanh(c0 * (y + 0.044715 * (y * y * y))))
    g = jnp.concatenate([g_ref[gb] for gb in range(NGB)], axis=1)
    gate = jnp.dot(g.astype(BF16), w_ref[...], preferred_element_type=F32) + b_ref[...]
    o_ref[...] = x_ref[...] + g * (1.0 / (1.0 + jnp.exp(-gate)))


def _s5_prep_kernel(lr_ref, li_ref, ldt_ref, be_re_ref, be_im_ref, ce_re_ref, ce_im_ref, d_ref,
                    w_ref, vt_ref, r_ref, pw_ref, wf_ref):
    nst = lr_ref.shape[2]
    lr, li = lr_ref[0], li_ref[0]
    dt = jnp.exp(ldt_ref[0])
    th_re, th_im = lr * dt, li * dt

    def apow(n):
        mag = jnp.exp(n * th_re)
        return mag * jnp.cos(n * th_im), mag * jnp.sin(n * th_im)

    n_small = lax.broadcasted_iota(jnp.int32, (3 * SUBLANES, 1), 0).astype(F32)
    p_re, p_im = apow(n_small)
    a_re, a_im = p_re[1:2], p_im[1:2]
    den = lr * lr + li * li
    z_re = ((a_re - 1.0) * lr + a_im * li) / den
    z_im = (a_im * lr - (a_re - 1.0) * li) / den
    bb_re, bb_im = _cmul(z_re, z_im, be_re_ref[0], be_im_ref[0])
    ce_re, ce_im = ce_re_ref[0], ce_im_ref[0]

    for s in range(Q):
        rows = slice(s * LANES, (s + 1) * LANES)
        n = Q - 1 - s
        w_re, w_im = _cmul(bb_re, bb_im, p_re[n:n + 1], p_im[n:n + 1])
        wf_ref[rows, :nst] = w_re
        wf_ref[rows, nst:] = w_im
        w_ref[0, rows, :nst] = w_re.astype(BF16)
        w_ref[0, rows, nst:] = w_im.astype(BF16)
        v_re, v_im = _cmul(ce_re, ce_im, p_re[s + 1:s + 2], p_im[s + 1:s + 2])
        vt_ref[0, rows, :nst] = v_re.astype(BF16)
        vt_ref[0, rows, nst:] = (-v_im).astype(BF16)

    dn = (((1,), (1,)), ((), ()))
    hi = lax.Precision.HIGHEST
    kall = (lax.dot_general(wf_ref[:, :nst], ce_re, dn, precision=hi, preferred_element_type=F32)
            - lax.dot_general(wf_ref[:, nst:], ce_im, dn, precision=hi, preferred_element_type=F32))
    ri = lax.broadcasted_iota(jnp.int32, (LANES, LANES), 0)
    ci = lax.broadcasted_iota(jnp.int32, (LANES, LANES), 1)
    lag0 = kall[(Q - 1) * LANES:] + jnp.where(ri == ci, d_ref[0], 0.0)
    kall = jnp.concatenate([kall[:(Q - 1) * LANES], lag0], axis=0)
    r_ref[0, :, LANES:] = kall.astype(BF16)
    r_ref[0, :, :LANES] = jnp.concatenate([kall[LANES:], jnp.zeros((LANES, LANES), F32)], axis=0).astype(BF16)

    n_big = float(Q) * lax.broadcasted_iota(jnp.int32, (pw_ref.shape[1], 1), 0).astype(F32)
    t_re, t_im = apow(n_big)
    pw_ref[0, :, :nst] = t_re
    pw_ref[0, :, nst:] = t_im


def _s5_tables(lam_re, lam_im, log_dt, b_re, b_im, c_re, c_im, d_skip):
    G, P = lam_re.shape
    C = b_re.shape[-1]
    nst = GPB * P
    eye = jnp.eye(GPB, dtype=bool)

    def expand(t):
        t = t.astype(F32).reshape(NGB, GPB, C, P)
        t = jnp.where(eye[None, :, None, :, None], t[:, :, :, None, :], 0.0)
        return t.reshape(NGB, GPB * C, nst)

    def lanes(t):
        return t.astype(F32).reshape(NGB, 1, nst)

    args = (lanes(lam_re), lanes(lam_im), lanes(jnp.broadcast_to(log_dt[:, None], (G, P))),
            expand(jnp.swapaxes(b_re, 1, 2)), expand(jnp.swapaxes(b_im, 1, 2)), expand(c_re), expand(c_im),
            d_skip.astype(F32).reshape(NGB, 1, LANES))
    row_spec = pl.BlockSpec((1, 1, nst), lambda gb: (gb, 0, 0))
    mat_spec = pl.BlockSpec((1, LANES, nst), lambda gb: (gb, 0, 0))
    n_pw = SEG + SUBLANES
    return pl.pallas_call(
        _s5_prep_kernel,
        grid=(NGB,),
        in_specs=[row_spec, row_spec, row_spec, mat_spec, mat_spec, mat_spec, mat_spec,
                  pl.BlockSpec((1, 1, LANES), lambda gb: (gb, 0, 0))],
        out_specs=[
            pl.BlockSpec((1, Q * LANES, 2 * nst), lambda gb: (gb, 0, 0)),
            pl.BlockSpec((1, Q * LANES, 2 * nst), lambda gb: (gb, 0, 0)),
            pl.BlockSpec((1, Q * LANES, MXU_DIM), lambda gb: (gb, 0, 0)),
            pl.BlockSpec((1, n_pw, 2 * nst), lambda gb: (gb, 0, 0)),
        ],
        out_shape=[
            jax.ShapeDtypeStruct((NGB, Q * LANES, 2 * nst), BF16),
            jax.ShapeDtypeStruct((NGB, Q * LANES, 2 * nst), BF16),
            jax.ShapeDtypeStruct((NGB, Q * LANES, MXU_DIM), BF16),
            jax.ShapeDtypeStruct((NGB, n_pw, 2 * nst), F32),
        ],
        scratch_shapes=[pltpu.VMEM((Q * LANES, 2 * nst), F32)],
        compiler_params=pltpu.CompilerParams(dimension_semantics=("arbitrary",), vmem_limit_bytes=VMEM_LIMIT),
        name="s5_prep",
    )(*args)


def _s5_layer(x, g, lam_re, lam_im, log_dt, b_re, b_im, c_re, c_im, d_skip, w_glu, b_glu, *, ma=64, mc=32):
    bsz, L, d = x.shape
    rows = bsz * L // Q
    rows_b = L // Q
    assert rows_b == NSEG * SEG and d == NGB * LANES
    w, vt, r_rev, pw = _s5_tables(lam_re, lam_im, log_dt, b_re, b_im, c_re, c_im, d_skip)
    xt = x.reshape(bsz * L, d)
    cp = pltpu.CompilerParams(dimension_semantics=("arbitrary",), vmem_limit_bytes=VMEM_LIMIT)

    h = pl.pallas_call(
        _s5_norm_kernel,
        grid=(rows // ma,),
        in_specs=[pl.BlockSpec((ma * Q, d), lambda i: (i, 0)), _const_spec((1, d))],
        out_specs=pl.BlockSpec((NGB, ma, Q * LANES), lambda i: (0, i, 0)),
        out_shape=jax.ShapeDtypeStruct((NGB, rows, Q * LANES), BF16),
        scratch_shapes=[pltpu.VMEM((NGB, ma * Q, LANES), F32)],
        compiler_params=cp,
        name="s5_norm",
    )(xt, g.reshape(1, d))

    nstate = w.shape[-1]
    nslab = nstate // LANES
    y = pl.pallas_call(
        _s5_core_kernel,
        grid=(NGB, bsz),
        in_specs=[
            pl.BlockSpec((1, rows_b, Q * LANES), lambda gb, b: (gb, b, 0)),
            pl.BlockSpec((1, Q * LANES, nstate), lambda gb, b: (gb, 0, 0)),
            pl.BlockSpec((1, Q * LANES, MXU_DIM), lambda gb, b: (gb, 0, 0)),
            pl.BlockSpec((1, Q * LANES, nstate), lambda gb, b: (gb, 0, 0)),
            pl.BlockSpec((1, pw.shape[1], nstate), lambda gb, b: (gb, 0, 0)),
        ],
        out_specs=pl.BlockSpec((1, rows_b, Q * LANES), lambda gb, b: (gb, b, 0)),
        out_shape=jax.ShapeDtypeStruct((NGB, rows, Q * LANES), BF16),
        scratch_shapes=[
            pltpu.VMEM((nslab, NSEG * PITCH, LANES), F32),
            pltpu.VMEM((nslab, NSEG * PITCH, LANES), F32),
        ],
        compiler_params=pltpu.CompilerParams(
            dimension_semantics=("arbitrary", "arbitrary"), vmem_limit_bytes=VMEM_LIMIT),
        name="s5_core",
    )(h, w, r_rev, vt, pw)

    out = pl.pallas_call(
        _s5_out_kernel,
        grid=(rows // mc,),
        in_specs=[
            pl.BlockSpec((mc * Q, d), lambda i: (i, 0)),
            pl.BlockSpec((NGB, mc, Q * LANES), lambda i: (0, i, 0)),
            _const_spec((d, d)),
            _const_spec((1, d)),
        ],
        out_specs=pl.BlockSpec((mc * Q, d), lambda i: (i, 0)),
        out_shape=jax.ShapeDtypeStruct((bsz * L, d), F32),
        scratch_shapes=[pltpu.VMEM((NGB, mc * Q, LANES), F32)],
        compiler_params=cp,
        name="s5_out",
    )(xt, y, w_glu.astype(BF16), b_glu.reshape(1, d))
    return out.reshape(bsz, L, d)


def kernel(x, positions, norm_mix, norm_ffn, norm_final, s5_lambda_re, s5_lambda_im, s5_log_dt, s5_b_re, s5_b_im, s5_c_re, s5_c_im, s5_d, s5_w_glu, s5_b_glu, attn_w_qkv, attn_b_qkv, attn_sinks, attn_w_o, attn_b_o, ffn_w_up, ffn_w_conv, ffn_b_conv, ffn_w_down):
    depth = norm_mix.shape[0]
    for i in range(depth):
        j = i // 2
        if i % 2 == 0:
            x = _s5_layer(x, norm_mix[i], s5_lambda_re[j], s5_lambda_im[j], s5_log_dt[j],
                          s5_b_re[j], s5_b_im[j], s5_c_re[j], s5_c_im[j], s5_d[j], s5_w_glu[j], s5_b_glu[j])
        else:
            x = _swa_layer(x, positions, norm_mix[i], attn_w_qkv[j], attn_b_qkv[j], attn_sinks[j],
                           attn_w_o[j], attn_b_o[j])
        x = _conv_ffn(x, norm_ffn[i], ffn_w_up[i], ffn_w_conv[i], ffn_b_conv[i], ffn_w_down[i],
                      norm_final if i == depth - 1 else None)
    return x
```

```python
import functools

import numpy as np
import jax
import jax.numpy as jnp
from jax import lax
from jax.experimental import pallas as pl
from jax.experimental.pallas import tpu as pltpu

F32 = jnp.float32
BF16 = jnp.bfloat16

EPS = 1e-5
NEG_INF = -1e30
GELU_C0 = float(np.sqrt(2.0 / np.pi).astype(np.float32))

HEAD_DIM = 64
N_KV_HEADS = 4
ROPE_DIM = 16
ROPE_THETA = 500000.0
ATTN_BLOCK = 128
S5_GROUP = 16

LANES = 128
SUBLANES = 8
MXU_DIM = 256
VMEM_LIMIT = 56 * 1024 * 1024

Q = 16
NGB = 8
GPB = LANES // S5_GROUP
SEG = 64
NSEG = 8
PITCH = 72


def _rms(x, g):
    ms = jnp.mean(x * x, axis=-1, keepdims=True)
    return x * lax.rsqrt(ms + EPS) * g


def _cmul(ar, ai, br, bi):
    return ar * br - ai * bi, ar * bi + ai * br


def _const_spec(shape):
    nd = len(shape)
    return pl.BlockSpec(shape, lambda *_: (0,) * nd, pipeline_mode=pl.Buffered(1))


def _ffn_kernel(x_ref, g_ref, wu_ref, wc_ref, bc_ref, wd_ref, gf_ref, o_ref, act_ref, carry_ref,
                *, fb, final_norm):
    tm = x_ref.shape[1]
    d_ff = wd_ref.shape[0]

    @pl.when(pl.program_id(1) == 0)
    def _():
        carry_ref[...] = jnp.zeros_like(carry_ref)

    x = x_ref[0]
    h = _rms(x, g_ref[...]).astype(BF16)
    rows = lax.broadcasted_iota(jnp.int32, (SUBLANES, fb), 0)

    def conv_block(col):
        u = jnp.dot(h, wu_ref[:, col:col + fb], preferred_element_type=F32)
        prev = carry_ref[:, col:col + fb]
        carry_ref[:, col:col + fb] = u[tm - SUBLANES:, :]
        s1 = pltpu.roll(u, 1, 0)
        s2 = pltpu.roll(u, 2, 0)
        t1 = jnp.where(rows < 1, pltpu.roll(prev, 1, 0), s1[:SUBLANES])
        t2 = jnp.where(rows < 2, pltpu.roll(prev, 2, 0), s2[:SUBLANES])
        s1 = jnp.concatenate([t1, s1[SUBLANES:]], axis=0)
        s2 = jnp.concatenate([t2, s2[SUBLANES:]], axis=0)
        w = wc_ref[:, col:col + fb]
        return w[0:1] * s2 + w[1:2] * s1 + w[2:3] * u + bc_ref[:, col:col + fb]

    for j in range(d_ff // fb):
        a = conv_block(j * fb)
        v = conv_block(d_ff + j * fb)
        act_ref[:, j * fb:(j + 1) * fb] = (a * (1.0 + jnp.tanh(a)) * v).astype(BF16)

    y = jnp.dot(act_ref[...], wd_ref[...], preferred_element_type=F32) + x
    if final_norm:
        y = _rms(y, gf_ref[...])
    o_ref[0] = y


def _layer_spec(shape, layer):
    nd = len(shape)
    return pl.BlockSpec((None,) + shape, lambda *_: (layer,) + (0,) * nd, pipeline_mode=pl.Buffered(1))


def _conv_ffn(x, g, w_up, w_conv, b_conv, w_down, layer, g_final, *, tm=1024, fb=256):
    bsz, L, d = x.shape
    d_ff = w_down.shape[1]
    final_norm = g_final is not None
    gf = g_final if final_norm else g
    kern = functools.partial(_ffn_kernel, fb=fb, final_norm=final_norm)
    half_gate = jnp.where(jnp.arange(2 * d_ff) < d_ff, 0.5, 1.0).astype(F32)
    w_conv = w_conv * half_gate[None, :]
    b_conv = b_conv * half_gate
    return pl.pallas_call(
        kern,
        grid=(bsz, L // tm),
        in_specs=[
            pl.BlockSpec((1, tm, d), lambda b, i: (b, i, 0)),
            _const_spec((1, d)),
            _layer_spec((d, 2 * d_ff), layer),
            _const_spec((3, 2 * d_ff)),
            _const_spec((1, 2 * d_ff)),
            _layer_spec((d_ff, d), layer),
            _const_spec((1, d)),
        ],
        out_specs=pl.BlockSpec((1, tm, d), lambda b, i: (b, i, 0)),
        out_shape=jax.ShapeDtypeStruct(x.shape, F32),
        scratch_shapes=[
            pltpu.VMEM((tm, d_ff), BF16),
            pltpu.VMEM((SUBLANES, 2 * d_ff), F32),
        ],
        compiler_params=pltpu.CompilerParams(
            dimension_semantics=("arbitrary", "arbitrary"), vmem_limit_bytes=VMEM_LIMIT),
        name="conv_ffn",
    )(x, g.reshape(1, d), w_up, w_conv, b_conv.reshape(1, -1), w_down, gf.reshape(1, d))


def _qkv_kernel(x_ref, pos_ref, g_ref, wt_ref, bt_ref, fr_ref, qt_ref, k_ref, vt_ref):
    d = x_ref.shape[2]
    nkv = N_KV_HEADS * HEAD_DIM
    half = ROPE_DIM // 2
    h = _rms(x_ref[0], g_ref[...]).astype(BF16)
    qkvt = lax.dot_general(wt_ref[...], h, (((1,), (1,)), ((), ())), preferred_element_type=F32) + bt_ref[...]
    ang = fr_ref[...] * pos_ref[0].astype(F32)
    cos = jnp.cos(ang)
    sin = jnp.sin(ang)

    def rotate(base):
        t1 = qkvt[base:base + half]
        t2 = qkvt[base + half:base + ROPE_DIM]
        return jnp.concatenate([t1 * cos - t2 * sin, t2 * cos + t1 * sin, qkvt[base + ROPE_DIM:base + HEAD_DIM]],
                               axis=0)

    for hq in range(d // HEAD_DIM):
        qt_ref[0, hq * HEAD_DIM:(hq + 1) * HEAD_DIM, :] = rotate(hq * HEAD_DIM).astype(BF16)
    kt = jnp.concatenate([rotate(d + hk * HEAD_DIM) for hk in range(N_KV_HEADS)], axis=0)
    k_ref[0] = kt.T.astype(BF16)
    vt_ref[0] = qkvt[d + nkv:].astype(BF16)


def _attn_kernel(sink_ref, qt_ref, kc_ref, kp_ref, vtc_ref, vtp_ref, x_ref, wo_ref, bo_ref, o_ref, ot_ref):
    tq = x_ref.shape[1]
    blk = ATTN_BLOCK
    q_per_kv = qt_ref.shape[1] // (N_KV_HEADS * HEAD_DIM)
    first_tile = pl.program_id(1) == 0
    key = lax.broadcasted_iota(jnp.int32, (2 * blk, q_per_kv * blk), 0)
    col = lax.broadcasted_iota(jnp.int32, (2 * blk, q_per_kv * blk), 1)
    qi = col & (blk - 1)
    valid = jnp.logical_or(jnp.logical_and(key < blk, key > qi), jnp.logical_and(key >= blk, key - blk <= qi))
    valid_first = jnp.logical_and(valid, jnp.logical_or(key >= blk, jnp.logical_not(first_tile)))
    fills = []
    for hk in range(N_KV_HEADS):
        sk = jnp.full(key.shape, NEG_INF, F32)
        for g in range(q_per_kv):
            in_head = jnp.logical_and(key == 0, jnp.logical_and(col >= g * blk, col < (g + 1) * blk))
            sk = jnp.where(in_head, sink_ref[hk * q_per_kv + g], sk)
        fills.append(sk)
    key_lane = lax.broadcasted_iota(jnp.int32, (HEAD_DIM, 2 * blk), 1)
    def scores(n, hk):
        cols = slice(n * blk, (n + 1) * blk)
        if n == 0:
            k2 = jnp.concatenate([kp_ref[0], kc_ref[0, cols, :]], axis=0)
            mask = valid_first
        else:
            k2 = kc_ref[0, (n - 1) * blk:(n + 1) * blk, :]
            mask = valid
        heads = [hk * q_per_kv + g for g in range(q_per_kv)]
        kk = k2[:, hk * HEAD_DIM:(hk + 1) * HEAD_DIM]
        qc = jnp.concatenate([qt_ref[0, hq * HEAD_DIM:(hq + 1) * HEAD_DIM, cols] for hq in heads], axis=1)
        return jnp.where(mask, jnp.dot(kk, qc, preferred_element_type=F32), fills[hk])

    def finish(n, hk, s):
        cols = slice(n * blk, (n + 1) * blk)
        if n == 0:
            v2t = jnp.concatenate([vtp_ref[0], vtc_ref[0, :, cols]], axis=1)
        else:
            v2t = vtc_ref[0, :, (n - 1) * blk:(n + 1) * blk]
        heads = [hk * q_per_kv + g for g in range(q_per_kv)]
        vv = v2t[hk * HEAD_DIM:(hk + 1) * HEAD_DIM, :]
        vv = jnp.where(key_lane == 0, jnp.zeros_like(vv), vv)
        p = jnp.exp(s - jnp.max(s, axis=0, keepdims=True))
        inv = 1.0 / jnp.sum(p, axis=0, keepdims=True)
        ot = jnp.dot(vv, p.astype(BF16), preferred_element_type=F32) * inv
        for g, hq in enumerate(heads):
            ot_ref[hq * HEAD_DIM:(hq + 1) * HEAD_DIM, cols] = ot[:, g * blk:(g + 1) * blk].astype(BF16)

    units = [(n, hk) for n in range(tq // blk) for hk in range(N_KV_HEADS)]
    s_cur = scores(*units[0])
    for i, unit in enumerate(units):
        s_next = scores(*units[i + 1]) if i + 1 < len(units) else None
        finish(*unit, s_cur)
        s_cur = s_next
    attn = lax.dot_general(ot_ref[...], wo_ref[...], (((0,), (0,)), ((), ())), preferred_element_type=F32)
    o_ref[0] = attn + bo_ref[...] + x_ref[0]


def _swa_layer(x, pos, g, w_qkv, b_qkv, sinks, w_o, b_o, *, tm=512, tq=512):
    bsz, L, d = x.shape
    nkv = N_KV_HEADS * HEAD_DIM
    qkv_dim = w_qkv.shape[1]
    inv_freq = 1.0 / jnp.power(ROPE_THETA, jnp.arange(0, ROPE_DIM, 2, dtype=F32) / ROPE_DIM)
    scale = jnp.where(jnp.arange(qkv_dim) < d, HEAD_DIM ** -0.5, 1.0).astype(F32)
    wt = (w_qkv * scale[None, :]).T.astype(BF16)
    bt = (b_qkv * scale).astype(F32).reshape(qkv_dim, 1)

    qt, k, vt = pl.pallas_call(
        _qkv_kernel,
        grid=(bsz, L // tm),
        in_specs=[
            pl.BlockSpec((1, tm, d), lambda b, i: (b, i, 0)),
            pl.BlockSpec((1, 1, tm), lambda b, i: (b, 0, i)),
            _const_spec((1, d)),
            _const_spec((qkv_dim, d)),
            _const_spec((qkv_dim, 1)),
            _const_spec((ROPE_DIM // 2, 1)),
        ],
        out_specs=[
            pl.BlockSpec((1, d, tm), lambda b, i: (b, 0, i)),
            pl.BlockSpec((1, tm, nkv), lambda b, i: (b, i, 0)),
            pl.BlockSpec((1, nkv, tm), lambda b, i: (b, 0, i)),
        ],
        out_shape=[
            jax.ShapeDtypeStruct((bsz, d, L), BF16),
            jax.ShapeDtypeStruct((bsz, L, nkv), BF16),
            jax.ShapeDtypeStruct((bsz, nkv, L), BF16),
        ],
        compiler_params=pltpu.CompilerParams(
            dimension_semantics=("arbitrary", "arbitrary"), vmem_limit_bytes=VMEM_LIMIT),
        name="qkv_rope",
    )(x, pos.reshape(bsz, 1, L), g.reshape(1, d), wt, bt, inv_freq.reshape(-1, 1))

    bpt = tq // ATTN_BLOCK
    prev_block = lambda i: jnp.maximum(i * bpt - 1, 0)
    return pl.pallas_call(
        _attn_kernel,
        grid=(bsz, L // tq),
        in_specs=[
            pl.BlockSpec(memory_space=pltpu.SMEM),
            pl.BlockSpec((1, d, tq), lambda b, i: (b, 0, i)),
            pl.BlockSpec((1, tq, nkv), lambda b, i: (b, i, 0)),
            pl.BlockSpec((1, ATTN_BLOCK, nkv), lambda b, i: (b, prev_block(i), 0)),
            pl.BlockSpec((1, nkv, tq), lambda b, i: (b, 0, i)),
            pl.BlockSpec((1, nkv, ATTN_BLOCK), lambda b, i: (b, 0, prev_block(i))),
            pl.BlockSpec((1, tq, d), lambda b, i: (b, i, 0)),
            _const_spec((d, d)),
            _const_spec((1, d)),
        ],
        out_specs=pl.BlockSpec((1, tq, d), lambda b, i: (b, i, 0)),
        out_shape=jax.ShapeDtypeStruct(x.shape, F32),
        scratch_shapes=[pltpu.VMEM((d, tq), BF16)],
        compiler_params=pltpu.CompilerParams(
            dimension_semantics=("arbitrary", "arbitrary"), vmem_limit_bytes=VMEM_LIMIT),
        name="swa_attn",
    )(sinks.astype(F32), qt, k, k, vt, vt, x, w_o.astype(BF16), b_o.reshape(1, d))


def _s5_norm_kernel(x_ref, g_ref, h_ref, hs_ref):
    ma = h_ref.shape[1]
    hs = _rms(x_ref[...], g_ref[...])
    for gb in range(NGB):
        hs_ref[gb] = hs[:, gb * LANES:(gb + 1) * LANES]
    for s in range(Q):
        for gb in range(NGB):
            h_ref[gb, :, s * LANES:(s + 1) * LANES] = hs_ref[gb, pl.ds(s, ma, stride=Q), :].astype(BF16)


def _s5_core_kernel(h_ref, w_ref, r_ref, vt_ref, pw_ref, y_ref, z_ref, sx_ref):
    nslab = z_ref.shape[0]
    nc = nslab // 2
    lhs = h_ref[0]
    z = jnp.dot(lhs, w_ref[0], preferred_element_type=F32)
    for j in range(NSEG):
        for c in range(nslab):
            z_ref[c, j * PITCH:j * PITCH + SEG, :] = z[j * SEG:(j + 1) * SEG, c * LANES:(c + 1) * LANES]

    half = nc * LANES

    def table(row0, nrows, c, imag):
        lo = (half if imag else 0) + c * LANES
        return pw_ref[0, row0:row0 + nrows, lo:lo + LANES]

    a_re = [jnp.broadcast_to(table(1, 1, c, False), (NSEG, LANES)) for c in range(nc)]
    a_im = [jnp.broadcast_to(table(1, 1, c, True), (NSEG, LANES)) for c in range(nc)]

    def scan_step(t, state):
        new_re, new_im = [], []
        for c in range(nc):
            s_re, s_im = state[c], state[c + nc]
            sx_ref[c, pl.ds(t, NSEG, stride=PITCH), :] = s_re
            sx_ref[c + nc, pl.ds(t, NSEG, stride=PITCH), :] = s_im
            z_re = z_ref[c, pl.ds(t, NSEG, stride=PITCH), :]
            z_im = z_ref[c + nc, pl.ds(t, NSEG, stride=PITCH), :]
            new_re.append(a_re[c] * s_re - a_im[c] * s_im + z_re)
            new_im.append(a_re[c] * s_im + a_im[c] * s_re + z_im)
        return tuple(new_re + new_im)

    zero = jnp.zeros((NSEG, LANES), F32)
    end = lax.fori_loop(0, SEG, scan_step, (zero,) * nslab)

    for c in range(nc):
        ap_re = table(SEG, 1, c, False)
        ap_im = table(SEG, 1, c, True)
        c_re = jnp.zeros((1, LANES), F32)
        c_im = jnp.zeros((1, LANES), F32)
        for j in range(1, NSEG):
            e_re = end[c][j - 1:j]
            e_im = end[c + nc][j - 1:j]
            c_re, c_im = (e_re + ap_re * c_re - ap_im * c_im, e_im + ap_re * c_im + ap_im * c_re)
            p_re = table(0, SEG, c, False)
            p_im = table(0, SEG, c, True)
            rows = slice(j * PITCH, j * PITCH + SEG)
            sx_ref[c, rows, :] = sx_ref[c, rows, :] + (p_re * c_re - p_im * c_im)
            sx_ref[c + nc, rows, :] = sx_ref[c + nc, rows, :] + (p_re * c_im + p_im * c_re)

    sx = jnp.concatenate(
        [jnp.concatenate([sx_ref[c, j * PITCH:j * PITCH + SEG, :] for c in range(nslab)], axis=1)
         for j in range(NSEG)], axis=0).astype(BF16)

    nt = lhs.shape[1] // MXU_DIM
    for b in range(nt):
        acc = jnp.dot(lhs[:, :(b + 1) * MXU_DIM], r_ref[0, (nt - 1 - b) * MXU_DIM:, :],
                      preferred_element_type=F32)
        acc = acc + lax.dot_general(sx, vt_ref[0, b * MXU_DIM:(b + 1) * MXU_DIM, :], (((1,), (1,)), ((), ())),
                                    preferred_element_type=F32)
        y_ref[0, :, b * MXU_DIM:(b + 1) * MXU_DIM] = acc.astype(BF16)


def _s5_out_kernel(x_ref, y_ref, w_ref, b_ref, o_ref, g_ref):
    mc = y_ref.shape[1]
    c0 = GELU_C0
    c1 = GELU_C0 * 0.044715
    for s in range(Q):
        for gb in range(NGB):
            y = y_ref[gb, :, s * LANES:(s + 1) * LANES].astype(F32)
            inner = y * (c0 + c1 * (y * y))
            g_ref[gb, pl.ds(s, mc, stride=Q), :] = (0.25 * y) * (1.0 + jnp.tanh(inner))
    gh = jnp.concatenate([g_ref[gb] for gb in range(NGB)], axis=1)
    half_gate = jnp.dot(gh.astype(BF16), w_ref[...], preferred_element_type=F32) + b_ref[...]
    o_ref[...] = x_ref[...] + gh * (1.0 + jnp.tanh(half_gate))


def _s5_prep_kernel(lr_ref, li_ref, ldt_ref, be_re_ref, be_im_ref, ce_re_ref, ce_im_ref, d_ref,
                    w_ref, vt_ref, r_ref, pw_ref, wf_ref):
    nst = lr_ref.shape[2]
    lr, li = lr_ref[0], li_ref[0]
    dt = jnp.exp(ldt_ref[0])
    th_re, th_im = lr * dt, li * dt

    def apow(n):
        mag = jnp.exp(n * th_re)
        return mag * jnp.cos(n * th_im), mag * jnp.sin(n * th_im)

    n_small = lax.broadcasted_iota(jnp.int32, (3 * SUBLANES, 1), 0).astype(F32)
    p_re, p_im = apow(n_small)
    a_re, a_im = p_re[1:2], p_im[1:2]
    den = lr * lr + li * li
    z_re = ((a_re - 1.0) * lr + a_im * li) / den
    z_im = (a_im * lr - (a_re - 1.0) * li) / den
    bb_re, bb_im = _cmul(z_re, z_im, be_re_ref[0], be_im_ref[0])
    ce_re, ce_im = ce_re_ref[0], ce_im_ref[0]

    for s in range(Q):
        rows = slice(s * LANES, (s + 1) * LANES)
        n = Q - 1 - s
        w_re, w_im = _cmul(bb_re, bb_im, p_re[n:n + 1], p_im[n:n + 1])
        wf_ref[rows, :nst] = w_re
        wf_ref[rows, nst:] = w_im
        w_ref[0, rows, :nst] = w_re.astype(BF16)
        w_ref[0, rows, nst:] = w_im.astype(BF16)
        v_re, v_im = _cmul(ce_re, ce_im, p_re[s + 1:s + 2], p_im[s + 1:s + 2])
        vt_ref[0, rows, :nst] = v_re.astype(BF16)
        vt_ref[0, rows, nst:] = (-v_im).astype(BF16)

    dn = (((1,), (1,)), ((), ()))
    hi = lax.Precision.HIGHEST
    kall = (lax.dot_general(wf_ref[:, :nst], ce_re, dn, precision=hi, preferred_element_type=F32)
            - lax.dot_general(wf_ref[:, nst:], ce_im, dn, precision=hi, preferred_element_type=F32))
    ri = lax.broadcasted_iota(jnp.int32, (LANES, LANES), 0)
    ci = lax.broadcasted_iota(jnp.int32, (LANES, LANES), 1)
    lag0 = kall[(Q - 1) * LANES:] + jnp.where(ri == ci, d_ref[0], 0.0)
    kall = jnp.concatenate([kall[:(Q - 1) * LANES], lag0], axis=0)
    r_ref[0, :, LANES:] = kall.astype(BF16)
    r_ref[0, :, :LANES] = jnp.concatenate([kall[LANES:], jnp.zeros((LANES, LANES), F32)], axis=0).astype(BF16)

    n_big = float(Q) * lax.broadcasted_iota(jnp.int32, (pw_ref.shape[1], 1), 0).astype(F32)
    t_re, t_im = apow(n_big)
    pw_ref[0, :, :nst] = t_re
    pw_ref[0, :, nst:] = t_im


def _s5_tables(lam_re, lam_im, log_dt, b_re, b_im, c_re, c_im, d_skip):
    G, P = lam_re.shape
    C = b_re.shape[-1]
    nst = GPB * P
    eye = jnp.eye(GPB, dtype=bool)

    def expand(t):
        t = t.astype(F32).reshape(NGB, GPB, C, P)
        t = jnp.where(eye[None, :, None, :, None], t[:, :, :, None, :], 0.0)
        return t.reshape(NGB, GPB * C, nst)

    def lanes(t):
        return t.astype(F32).reshape(NGB, 1, nst)

    args = (lanes(lam_re), lanes(lam_im), lanes(jnp.broadcast_to(log_dt[:, None], (G, P))),
            expand(jnp.swapaxes(b_re, 1, 2)), expand(jnp.swapaxes(b_im, 1, 2)), expand(c_re), expand(c_im),
            d_skip.astype(F32).reshape(NGB, 1, LANES))
    row_spec = pl.BlockSpec((1, 1, nst), lambda gb: (gb, 0, 0))
    mat_spec = pl.BlockSpec((1, LANES, nst), lambda gb: (gb, 0, 0))
    n_pw = SEG + SUBLANES
    return pl.pallas_call(
        _s5_prep_kernel,
        grid=(NGB,),
        in_specs=[row_spec, row_spec, row_spec, mat_spec, mat_spec, mat_spec, mat_spec,
                  pl.BlockSpec((1, 1, LANES), lambda gb: (gb, 0, 0))],
        out_specs=[
            pl.BlockSpec((1, Q * LANES, 2 * nst), lambda gb: (gb, 0, 0)),
            pl.BlockSpec((1, Q * LANES, 2 * nst), lambda gb: (gb, 0, 0)),
            pl.BlockSpec((1, Q * LANES, MXU_DIM), lambda gb: (gb, 0, 0)),
            pl.BlockSpec((1, n_pw, 2 * nst), lambda gb: (gb, 0, 0)),
        ],
        out_shape=[
            jax.ShapeDtypeStruct((NGB, Q * LANES, 2 * nst), BF16),
            jax.ShapeDtypeStruct((NGB, Q * LANES, 2 * nst), BF16),
            jax.ShapeDtypeStruct((NGB, Q * LANES, MXU_DIM), BF16),
            jax.ShapeDtypeStruct((NGB, n_pw, 2 * nst), F32),
        ],
        scratch_shapes=[pltpu.VMEM((Q * LANES, 2 * nst), F32)],
        compiler_params=pltpu.CompilerParams(dimension_semantics=("arbitrary",), vmem_limit_bytes=VMEM_LIMIT),
        name="s5_prep",
    )(*args)


def _s5_layer(x, g, lam_re, lam_im, log_dt, b_re, b_im, c_re, c_im, d_skip, w_glu, b_glu, *, ma=64, mc=32):
    bsz, L, d = x.shape
    rows = bsz * L // Q
    rows_b = L // Q
    assert rows_b == NSEG * SEG and d == NGB * LANES
    w, vt, r_rev, pw = _s5_tables(lam_re, lam_im, log_dt, b_re, b_im, c_re, c_im, d_skip)
    xt = x.reshape(bsz * L, d)
    cp = pltpu.CompilerParams(dimension_semantics=("arbitrary",), vmem_limit_bytes=VMEM_LIMIT)

    h = pl.pallas_call(
        _s5_norm_kernel,
        grid=(rows // ma,),
        in_specs=[pl.BlockSpec((ma * Q, d), lambda i: (i, 0)), _const_spec((1, d))],
        out_specs=pl.BlockSpec((NGB, ma, Q * LANES), lambda i: (0, i, 0)),
        out_shape=jax.ShapeDtypeStruct((NGB, rows, Q * LANES), BF16),
        scratch_shapes=[pltpu.VMEM((NGB, ma * Q, LANES), F32)],
        compiler_params=cp,
        name="s5_norm",
    )(xt, g.reshape(1, d))

    nstate = w.shape[-1]
    nslab = nstate // LANES
    y = pl.pallas_call(
        _s5_core_kernel,
        grid=(NGB, bsz),
        in_specs=[
            pl.BlockSpec((1, rows_b, Q * LANES), lambda gb, b: (gb, b, 0)),
            pl.BlockSpec((1, Q * LANES, nstate), lambda gb, b: (gb, 0, 0)),
            pl.BlockSpec((1, Q * LANES, MXU_DIM), lambda gb, b: (gb, 0, 0)),
            pl.BlockSpec((1, Q * LANES, nstate), lambda gb, b: (gb, 0, 0)),
            pl.BlockSpec((1, pw.shape[1], nstate), lambda gb, b: (gb, 0, 0)),
        ],
        out_specs=pl.BlockSpec((1, rows_b, Q * LANES), lambda gb, b: (gb, b, 0)),
        out_shape=jax.ShapeDtypeStruct((NGB, rows, Q * LANES), BF16),
        scratch_shapes=[
            pltpu.VMEM((nslab, NSEG * PITCH, LANES), F32),
            pltpu.VMEM((nslab, NSEG * PITCH, LANES), F32),
        ],
        compiler_params=pltpu.CompilerParams(
            dimension_semantics=("arbitrary", "arbitrary"), vmem_limit_bytes=VMEM_LIMIT),
        name="s5_core",
    )(h, w, r_rev, vt, pw)

    out = pl.pallas_call(
        _s5_out_kernel,
        grid=(rows // mc,),
        in_specs=[
            pl.BlockSpec((mc * Q, d), lambda i: (i, 0)),
            pl.BlockSpec((NGB, mc, Q * LANES), lambda i: (0, i, 0)),
            _const_spec((d, d)),
            _const_spec((1, d)),
        ],
        out_specs=pl.BlockSpec((mc * Q, d), lambda i: (i, 0)),
        out_shape=jax.ShapeDtypeStruct((bsz * L, d), F32),
        scratch_shapes=[pltpu.VMEM((NGB, mc * Q, LANES), F32)],
        compiler_params=cp,
        name="s5_out",
    )(xt, y, w_glu.astype(BF16), (0.5 * b_glu).reshape(1, d))
    return out.reshape(bsz, L, d)


def kernel(x, positions, norm_mix, norm_ffn, norm_final, s5_lambda_re, s5_lambda_im, s5_log_dt, s5_b_re, s5_b_im, s5_c_re, s5_c_im, s5_d, s5_w_glu, s5_b_glu, attn_w_qkv, attn_b_qkv, attn_sinks, attn_w_o, attn_b_o, ffn_w_up, ffn_w_conv, ffn_b_conv, ffn_w_down):
    depth = norm_mix.shape[0]
    w_up = ffn_w_up.astype(BF16)
    w_down = ffn_w_down.astype(BF16)
    for i in range(depth):
        j = i // 2
        if i % 2 == 0:
            x = _s5_layer(x, norm_mix[i], s5_lambda_re[j], s5_lambda_im[j], s5_log_dt[j],
                          s5_b_re[j], s5_b_im[j], s5_c_re[j], s5_c_im[j], s5_d[j], s5_w_glu[j], s5_b_glu[j])
        else:
            x = _swa_layer(x, positions, norm_mix[i], attn_w_qkv[j], attn_b_qkv[j], attn_sinks[j],
                           attn_w_o[j], attn_b_o[j])
        x = _conv_ffn(x, norm_ffn[i], w_up, ffn_w_conv[i], ffn_b_conv[i], w_down, i,
                      norm_final if i == depth - 1 else None)
    return x
```

```python
import functools

import numpy as np
import jax
import jax.numpy as jnp
from jax import lax
from jax.experimental import pallas as pl
from jax.experimental.pallas import tpu as pltpu

F32 = jnp.float32
BF16 = jnp.bfloat16

EPS = 1e-5
NEG_INF = -1e30
GELU_C0 = float(np.sqrt(2.0 / np.pi).astype(np.float32))

HEAD_DIM = 64
N_KV_HEADS = 4
ROPE_DIM = 16
ROPE_THETA = 500000.0
ATTN_BLOCK = 128
S5_GROUP = 16

LANES = 128
SUBLANES = 8
MXU_DIM = 256
VMEM_LIMIT = 56 * 1024 * 1024

Q = 16
NGB = 8
GPB = LANES // S5_GROUP
NSUB = 4
SUBW = LANES // NSUB
TPT = MXU_DIM // SUBW
SEG = 64
NSEG = 8
PITCH = 72


def _rms(x, g):
    ms = jnp.mean(x * x, axis=-1, keepdims=True)
    return x * lax.rsqrt(ms + EPS) * g


def _cmul(ar, ai, br, bi):
    return ar * br - ai * bi, ar * bi + ai * br


def _const_spec(shape):
    nd = len(shape)
    return pl.BlockSpec(shape, lambda *_: (0,) * nd, pipeline_mode=pl.Buffered(1))


def _ffn_kernel(x_ref, g_ref, wu_ref, wc_ref, bc_ref, wd_ref, gf_ref, o_ref, act_ref, carry_ref,
                *, fb, final_norm):
    tm = x_ref.shape[1]
    d_ff = wd_ref.shape[0]

    @pl.when(pl.program_id(1) == 0)
    def _():
        carry_ref[...] = jnp.zeros_like(carry_ref)

    x = x_ref[0]
    h = _rms(x, g_ref[...]).astype(BF16)
    rows = lax.broadcasted_iota(jnp.int32, (SUBLANES, fb), 0)

    def conv_block(col):
        u = jnp.dot(h, wu_ref[:, col:col + fb], preferred_element_type=F32)
        prev = carry_ref[:, col:col + fb]
        carry_ref[:, col:col + fb] = u[tm - SUBLANES:, :]
        s1 = pltpu.roll(u, 1, 0)
        s2 = pltpu.roll(u, 2, 0)
        t1 = jnp.where(rows < 1, pltpu.roll(prev, 1, 0), s1[:SUBLANES])
        t2 = jnp.where(rows < 2, pltpu.roll(prev, 2, 0), s2[:SUBLANES])
        s1 = jnp.concatenate([t1, s1[SUBLANES:]], axis=0)
        s2 = jnp.concatenate([t2, s2[SUBLANES:]], axis=0)
        w = wc_ref[:, col:col + fb]
        return w[0:1] * s2 + w[1:2] * s1 + w[2:3] * u + bc_ref[:, col:col + fb]

    for j in range(d_ff // fb):
        a = conv_block(j * fb)
        v = conv_block(d_ff + j * fb)
        act_ref[:, j * fb:(j + 1) * fb] = (a * (1.0 + jnp.tanh(a)) * v).astype(BF16)

    y = jnp.dot(act_ref[...], wd_ref[...], preferred_element_type=F32) + x
    if final_norm:
        y = _rms(y, gf_ref[...])
    o_ref[0] = y


def _layer_spec(shape, layer):
    nd = len(shape)
    return pl.BlockSpec((None,) + shape, lambda *_: (layer,) + (0,) * nd, pipeline_mode=pl.Buffered(1))


def _conv_ffn(x, g, w_up, w_conv, b_conv, w_down, layer, g_final, *, tm=1024, fb=256):
    bsz, L, d = x.shape
    d_ff = w_down.shape[1]
    final_norm = g_final is not None
    gf = g_final if final_norm else g
    kern = functools.partial(_ffn_kernel, fb=fb, final_norm=final_norm)
    half_gate = jnp.where(jnp.arange(2 * d_ff) < d_ff, 0.5, 1.0).astype(F32)
    w_conv = w_conv * half_gate[None, :]
    b_conv = b_conv * half_gate
    return pl.pallas_call(
        kern,
        grid=(bsz, L // tm),
        in_specs=[
            pl.BlockSpec((1, tm, d), lambda b, i: (b, i, 0)),
            _const_spec((1, d)),
            _layer_spec((d, 2 * d_ff), layer),
            _const_spec((3, 2 * d_ff)),
            _const_spec((1, 2 * d_ff)),
            _layer_spec((d_ff, d), layer),
            _const_spec((1, d)),
        ],
        out_specs=pl.BlockSpec((1, tm, d), lambda b, i: (b, i, 0)),
        out_shape=jax.ShapeDtypeStruct(x.shape, F32),
        scratch_shapes=[
            pltpu.VMEM((tm, d_ff), BF16),
            pltpu.VMEM((SUBLANES, 2 * d_ff), F32),
        ],
        compiler_params=pltpu.CompilerParams(
            dimension_semantics=("arbitrary", "arbitrary"), vmem_limit_bytes=VMEM_LIMIT),
        name="conv_ffn",
    )(x, g.reshape(1, d), w_up, w_conv, b_conv.reshape(1, -1), w_down, gf.reshape(1, d))


def _qkv_kernel(x_ref, pos_ref, g_ref, wt_ref, bt_ref, fr_ref, qt_ref, k_ref, vt_ref):
    d = x_ref.shape[2]
    nkv = N_KV_HEADS * HEAD_DIM
    half = ROPE_DIM // 2
    h = _rms(x_ref[0], g_ref[...]).astype(BF16)
    qkvt = lax.dot_general(wt_ref[...], h, (((1,), (1,)), ((), ())), preferred_element_type=F32) + bt_ref[...]
    ang = fr_ref[...] * pos_ref[0].astype(F32)
    cos = jnp.cos(ang)
    sin = jnp.sin(ang)

    def rotate(base):
        t1 = qkvt[base:base + half]
        t2 = qkvt[base + half:base + ROPE_DIM]
        return jnp.concatenate([t1 * cos - t2 * sin, t2 * cos + t1 * sin, qkvt[base + ROPE_DIM:base + HEAD_DIM]],
                               axis=0)

    for hq in range(d // HEAD_DIM):
        qt_ref[0, hq * HEAD_DIM:(hq + 1) * HEAD_DIM, :] = rotate(hq * HEAD_DIM).astype(BF16)
    kt = jnp.concatenate([rotate(d + hk * HEAD_DIM) for hk in range(N_KV_HEADS)], axis=0)
    k_ref[0] = kt.T.astype(BF16)
    vt_ref[0] = qkvt[d + nkv:].astype(BF16)


def _attn_kernel(sink_ref, qt_ref, kc_ref, kp_ref, vtc_ref, vtp_ref, x_ref, wo_ref, bo_ref, o_ref, ot_ref):
    tq = x_ref.shape[1]
    blk = ATTN_BLOCK
    q_per_kv = qt_ref.shape[1] // (N_KV_HEADS * HEAD_DIM)
    first_tile = pl.program_id(1) == 0
    key = lax.broadcasted_iota(jnp.int32, (2 * blk, q_per_kv * blk), 0)
    col = lax.broadcasted_iota(jnp.int32, (2 * blk, q_per_kv * blk), 1)
    qi = col & (blk - 1)
    valid = jnp.logical_or(jnp.logical_and(key < blk, key > qi), jnp.logical_and(key >= blk, key - blk <= qi))
    valid_first = jnp.logical_and(valid, jnp.logical_or(key >= blk, jnp.logical_not(first_tile)))
    fills = []
    for hk in range(N_KV_HEADS):
        sk = jnp.full(key.shape, NEG_INF, F32)
        for g in range(q_per_kv):
            in_head = jnp.logical_and(key == 0, jnp.logical_and(col >= g * blk, col < (g + 1) * blk))
            sk = jnp.where(in_head, sink_ref[hk * q_per_kv + g], sk)
        fills.append(sk)
    key_lane = lax.broadcasted_iota(jnp.int32, (HEAD_DIM, 2 * blk), 1)
    def scores(n, hk):
        cols = slice(n * blk, (n + 1) * blk)
        if n == 0:
            k2 = jnp.concatenate([kp_ref[0], kc_ref[0, cols, :]], axis=0)
            mask = valid_first
        else:
            k2 = kc_ref[0, (n - 1) * blk:(n + 1) * blk, :]
            mask = valid
        heads = [hk * q_per_kv + g for g in range(q_per_kv)]
        kk = k2[:, hk * HEAD_DIM:(hk + 1) * HEAD_DIM]
        qc = jnp.concatenate([qt_ref[0, hq * HEAD_DIM:(hq + 1) * HEAD_DIM, cols] for hq in heads], axis=1)
        return jnp.where(mask, jnp.dot(kk, qc, preferred_element_type=F32), fills[hk])

    def finish(n, hk, s):
        cols = slice(n * blk, (n + 1) * blk)
        if n == 0:
            v2t = jnp.concatenate([vtp_ref[0], vtc_ref[0, :, cols]], axis=1)
        else:
            v2t = vtc_ref[0, :, (n - 1) * blk:(n + 1) * blk]
        heads = [hk * q_per_kv + g for g in range(q_per_kv)]
        vv = v2t[hk * HEAD_DIM:(hk + 1) * HEAD_DIM, :]
        vv = jnp.where(key_lane == 0, jnp.zeros_like(vv), vv)
        p = jnp.exp(s - jnp.max(s, axis=0, keepdims=True))
        inv = 1.0 / jnp.sum(p, axis=0, keepdims=True)
        ot = jnp.dot(vv, p.astype(BF16), preferred_element_type=F32) * inv
        for g, hq in enumerate(heads):
            ot_ref[hq * HEAD_DIM:(hq + 1) * HEAD_DIM, cols] = ot[:, g * blk:(g + 1) * blk].astype(BF16)

    units = [(n, hk) for n in range(tq // blk) for hk in range(N_KV_HEADS)]
    s_cur = scores(*units[0])
    for i, unit in enumerate(units):
        s_next = scores(*units[i + 1]) if i + 1 < len(units) else None
        finish(*unit, s_cur)
        s_cur = s_next
    attn = lax.dot_general(ot_ref[...], wo_ref[...], (((0,), (0,)), ((), ())), preferred_element_type=F32)
    o_ref[0] = attn + bo_ref[...] + x_ref[0]


def _swa_layer(x, pos, g, w_qkv, b_qkv, sinks, w_o, b_o, *, tm=512, tq=512):
    bsz, L, d = x.shape
    nkv = N_KV_HEADS * HEAD_DIM
    qkv_dim = w_qkv.shape[1]
    inv_freq = 1.0 / jnp.power(ROPE_THETA, jnp.arange(0, ROPE_DIM, 2, dtype=F32) / ROPE_DIM)
    scale = jnp.where(jnp.arange(qkv_dim) < d, HEAD_DIM ** -0.5, 1.0).astype(F32)
    wt = (w_qkv * scale[None, :]).T.astype(BF16)
    bt = (b_qkv * scale).astype(F32).reshape(qkv_dim, 1)

    qt, k, vt = pl.pallas_call(
        _qkv_kernel,
        grid=(bsz, L // tm),
        in_specs=[
            pl.BlockSpec((1, tm, d), lambda b, i: (b, i, 0)),
            pl.BlockSpec((1, 1, tm), lambda b, i: (b, 0, i)),
            _const_spec((1, d)),
            _const_spec((qkv_dim, d)),
            _const_spec((qkv_dim, 1)),
            _const_spec((ROPE_DIM // 2, 1)),
        ],
        out_specs=[
            pl.BlockSpec((1, d, tm), lambda b, i: (b, 0, i)),
            pl.BlockSpec((1, tm, nkv), lambda b, i: (b, i, 0)),
            pl.BlockSpec((1, nkv, tm), lambda b, i: (b, 0, i)),
        ],
        out_shape=[
            jax.ShapeDtypeStruct((bsz, d, L), BF16),
            jax.ShapeDtypeStruct((bsz, L, nkv), BF16),
            jax.ShapeDtypeStruct((bsz, nkv, L), BF16),
        ],
        compiler_params=pltpu.CompilerParams(
            dimension_semantics=("arbitrary", "arbitrary"), vmem_limit_bytes=VMEM_LIMIT),
        name="qkv_rope",
    )(x, pos.reshape(bsz, 1, L), g.reshape(1, d), wt, bt, inv_freq.reshape(-1, 1))

    bpt = tq // ATTN_BLOCK
    prev_block = lambda i: jnp.maximum(i * bpt - 1, 0)
    return pl.pallas_call(
        _attn_kernel,
        grid=(bsz, L // tq),
        in_specs=[
            pl.BlockSpec(memory_space=pltpu.SMEM),
            pl.BlockSpec((1, d, tq), lambda b, i: (b, 0, i)),
            pl.BlockSpec((1, tq, nkv), lambda b, i: (b, i, 0)),
            pl.BlockSpec((1, ATTN_BLOCK, nkv), lambda b, i: (b, prev_block(i), 0)),
            pl.BlockSpec((1, nkv, tq), lambda b, i: (b, 0, i)),
            pl.BlockSpec((1, nkv, ATTN_BLOCK), lambda b, i: (b, 0, prev_block(i))),
            pl.BlockSpec((1, tq, d), lambda b, i: (b, i, 0)),
            _const_spec((d, d)),
            _const_spec((1, d)),
        ],
        out_specs=pl.BlockSpec((1, tq, d), lambda b, i: (b, i, 0)),
        out_shape=jax.ShapeDtypeStruct(x.shape, F32),
        scratch_shapes=[pltpu.VMEM((d, tq), BF16)],
        compiler_params=pltpu.CompilerParams(
            dimension_semantics=("arbitrary", "arbitrary"), vmem_limit_bytes=VMEM_LIMIT),
        name="swa_attn",
    )(sinks.astype(F32), qt, k, k, vt, vt, x, w_o.astype(BF16), b_o.reshape(1, d))


def _s5_norm_kernel(x_ref, g_ref, h_ref, hs_ref):
    ma = h_ref.shape[1]
    hs = _rms(x_ref[...], g_ref[...])
    for gb in range(NGB):
        hs_ref[gb] = hs[:, gb * LANES:(gb + 1) * LANES]
    lane = lax.broadcasted_iota(jnp.int32, (ma, LANES), 1)
    for gb in range(NGB):
        for i in range(Q // NSUB):
            tok = [hs_ref[gb, pl.ds(NSUB * i + r, ma, stride=Q), :] for r in range(NSUB)]
            for u in range(NSUB):
                acc = None
                for r in range(NSUB):
                    t = tok[r] if r == u else pltpu.roll(tok[r], ((r - u) * SUBW) % LANES, 1)
                    acc = t if acc is None else jnp.where(
                        jnp.logical_and(lane >= r * SUBW, lane < (r + 1) * SUBW), t, acc)
                col = u * Q * SUBW + i * LANES
                h_ref[gb, :, col:col + LANES] = acc.astype(BF16)


def _s5_core_kernel(h_ref, w_ref, r_ref, vt_ref, pw_ref, y_ref, z_ref, sx_ref):
    nslab = z_ref.shape[0]
    nc = nslab // 2
    kw = Q * SUBW
    nst = w_ref.shape[3] // 2
    spu = nst // LANES
    lhs = [h_ref[0, :, u * kw:(u + 1) * kw] for u in range(NSUB)]
    for u in range(NSUB):
        z = jnp.dot(lhs[u], w_ref[0, u], preferred_element_type=F32)
        for j in range(NSEG):
            for c in range(2 * spu):
                slab = (nc if c >= spu else 0) + u * spu + c % spu
                z_ref[slab, j * PITCH:j * PITCH + SEG, :] = z[j * SEG:(j + 1) * SEG, c * LANES:(c + 1) * LANES]

    half = nc * LANES

    def table(row0, nrows, c, imag):
        lo = (half if imag else 0) + c * LANES
        return pw_ref[0, row0:row0 + nrows, lo:lo + LANES]

    a_re = [jnp.broadcast_to(table(1, 1, c, False), (NSEG, LANES)) for c in range(nc)]
    a_im = [jnp.broadcast_to(table(1, 1, c, True), (NSEG, LANES)) for c in range(nc)]

    def scan_step(t, state):
        new_re, new_im = [], []
        for c in range(nc):
            s_re, s_im = state[c], state[c + nc]
            sx_ref[c, pl.ds(t, NSEG, stride=PITCH), :] = s_re
            sx_ref[c + nc, pl.ds(t, NSEG, stride=PITCH), :] = s_im
            z_re = z_ref[c, pl.ds(t, NSEG, stride=PITCH), :]
            z_im = z_ref[c + nc, pl.ds(t, NSEG, stride=PITCH), :]
            new_re.append(a_re[c] * s_re - a_im[c] * s_im + z_re)
            new_im.append(a_re[c] * s_im + a_im[c] * s_re + z_im)
        return tuple(new_re + new_im)

    zero = jnp.zeros((NSEG, LANES), F32)
    end = lax.fori_loop(0, SEG, scan_step, (zero,) * nslab)

    for c in range(nc):
        ap_re = table(SEG, 1, c, False)
        ap_im = table(SEG, 1, c, True)
        c_re = jnp.zeros((1, LANES), F32)
        c_im = jnp.zeros((1, LANES), F32)
        for j in range(1, NSEG):
            e_re = end[c][j - 1:j]
            e_im = end[c + nc][j - 1:j]
            c_re, c_im = (e_re + ap_re * c_re - ap_im * c_im, e_im + ap_re * c_im + ap_im * c_re)
            p_re = table(0, SEG, c, False)
            p_im = table(0, SEG, c, True)
            rows = slice(j * PITCH, j * PITCH + SEG)
            sx_ref[c, rows, :] = sx_ref[c, rows, :] + (p_re * c_re - p_im * c_im)
            sx_ref[c + nc, rows, :] = sx_ref[c + nc, rows, :] + (p_re * c_im + p_im * c_re)

    nt = kw // MXU_DIM
    for u in range(NSUB):
        slabs = [u * spu + c for c in range(spu)] + [nc + u * spu + c for c in range(spu)]
        sx = jnp.concatenate(
            [jnp.concatenate([sx_ref[c, j * PITCH:j * PITCH + SEG, :] for c in slabs], axis=1)
             for j in range(NSEG)], axis=0).astype(BF16)
        for b in range(nt):
            acc = jnp.dot(lhs[u][:, :(b + 1) * MXU_DIM], r_ref[0, u, (nt - 1 - b) * MXU_DIM:, :],
                          preferred_element_type=F32)
            acc = acc + lax.dot_general(sx, vt_ref[0, u, b * MXU_DIM:(b + 1) * MXU_DIM, :],
                                        (((1,), (1,)), ((), ())), preferred_element_type=F32)
            y_ref[0, :, u * kw + b * MXU_DIM:u * kw + (b + 1) * MXU_DIM] = acc.astype(BF16)


def _s5_out_kernel(x_ref, y_ref, w_ref, b_ref, o_ref, g_ref):
    mc = y_ref.shape[1]
    c0 = GELU_C0
    c1 = GELU_C0 * 0.044715
    lane = lax.broadcasted_iota(jnp.int32, (mc, LANES), 1)
    for gb in range(NGB):
        for i in range(Q // NSUB):
            sub = [y_ref[gb, :, u * Q * SUBW + i * LANES:u * Q * SUBW + (i + 1) * LANES].astype(F32)
                   for u in range(NSUB)]
            for r in range(NSUB):
                y = None
                for u in range(NSUB):
                    t = sub[u] if u == r else pltpu.roll(sub[u], ((u - r) * SUBW) % LANES, 1)
                    y = t if y is None else jnp.where(
                        jnp.logical_and(lane >= u * SUBW, lane < (u + 1) * SUBW), t, y)
                inner = y * (c0 + c1 * (y * y))
                g_ref[gb, pl.ds(NSUB * i + r, mc, stride=Q), :] = (0.25 * y) * (1.0 + jnp.tanh(inner))
    gh = jnp.concatenate([g_ref[gb] for gb in range(NGB)], axis=1)
    half_gate = jnp.dot(gh.astype(BF16), w_ref[...], preferred_element_type=F32) + b_ref[...]
    o_ref[...] = x_ref[...] + gh * (1.0 + jnp.tanh(half_gate))


def _s5_prep_kernel(lr_ref, li_ref, ldt_ref, be_re_ref, be_im_ref, ce_re_ref, ce_im_ref, d_ref,
                    w_ref, vt_ref, r_ref, pw_ref, wf_ref):
    nst = lr_ref.shape[2]
    nsu = nst // NSUB
    lr, li = lr_ref[0], li_ref[0]
    dt = jnp.exp(ldt_ref[0])
    th_re, th_im = lr * dt, li * dt

    def apow(n):
        mag = jnp.exp(n * th_re)
        return mag * jnp.cos(n * th_im), mag * jnp.sin(n * th_im)

    n_small = lax.broadcasted_iota(jnp.int32, (3 * SUBLANES, 1), 0).astype(F32)
    p_re, p_im = apow(n_small)
    a_re, a_im = p_re[1:2], p_im[1:2]
    den = lr * lr + li * li
    z_re = ((a_re - 1.0) * lr + a_im * li) / den
    z_im = (a_im * lr - (a_re - 1.0) * li) / den
    bb_re, bb_im = _cmul(z_re, z_im, be_re_ref[0], be_im_ref[0])
    ce_re, ce_im = ce_re_ref[0], ce_im_ref[0]

    for s in range(Q):
        rows = slice(s * LANES, (s + 1) * LANES)
        n = Q - 1 - s
        w_re, w_im = _cmul(bb_re, bb_im, p_re[n:n + 1], p_im[n:n + 1])
        wf_ref[rows, :nst] = w_re
        wf_ref[rows, nst:] = w_im
        v_re, v_im = _cmul(ce_re, ce_im, p_re[s + 1:s + 2], p_im[s + 1:s + 2])
        for u in range(NSUB):
            src_rows = slice(u * SUBW, (u + 1) * SUBW)
            src_cols = slice(u * nsu, (u + 1) * nsu)
            dst_rows = slice(s * SUBW, (s + 1) * SUBW)
            w_ref[0, u, dst_rows, :nsu] = w_re[src_rows, src_cols].astype(BF16)
            w_ref[0, u, dst_rows, nsu:] = w_im[src_rows, src_cols].astype(BF16)
            vt_ref[0, u, dst_rows, :nsu] = v_re[src_rows, src_cols].astype(BF16)
            vt_ref[0, u, dst_rows, nsu:] = (-v_im[src_rows, src_cols]).astype(BF16)

    dn = (((1,), (1,)), ((), ()))
    hi = lax.Precision.HIGHEST
    kall = (lax.dot_general(wf_ref[:, :nst], ce_re, dn, precision=hi, preferred_element_type=F32)
            - lax.dot_general(wf_ref[:, nst:], ce_im, dn, precision=hi, preferred_element_type=F32))
    ri = lax.broadcasted_iota(jnp.int32, (LANES, LANES), 0)
    ci = lax.broadcasted_iota(jnp.int32, (LANES, LANES), 1)
    lag0 = kall[(Q - 1) * LANES:] + jnp.where(ri == ci, d_ref[0], 0.0)
    kall = jnp.concatenate([kall[:(Q - 1) * LANES], lag0], axis=0)

    rolled = [kall] + [pltpu.roll(kall, dd * SUBW, 1) for dd in range(1, NSUB)]
    lane = lax.broadcasted_iota(jnp.int32, (Q * SUBW, LANES), 1)
    for u in range(NSUB):
        sub_rows = [jnp.concatenate([x[s * LANES + u * SUBW:s * LANES + (u + 1) * SUBW] for s in range(Q)], axis=0)
                    for x in rolled]
        for piece in range(MXU_DIM // LANES):
            acc = None
            for r in range(NSUB):
                up = (TPT - 1 - (piece * NSUB + r)) * SUBW
                x = sub_rows[(r - u) % NSUB]
                if up:
                    x = jnp.concatenate([x[up:], jnp.zeros((up, LANES), F32)], axis=0)
                acc = x if acc is None else jnp.where(
                    jnp.logical_and(lane >= r * SUBW, lane < (r + 1) * SUBW), x, acc)
            r_ref[0, u, :, piece * LANES:(piece + 1) * LANES] = acc.astype(BF16)

    n_big = float(Q) * lax.broadcasted_iota(jnp.int32, (pw_ref.shape[1], 1), 0).astype(F32)
    t_re, t_im = apow(n_big)
    pw_ref[0, :, :nst] = t_re
    pw_ref[0, :, nst:] = t_im


def _s5_tables(lam_re, lam_im, log_dt, b_re, b_im, c_re, c_im, d_skip):
    G, P = lam_re.shape
    C = b_re.shape[-1]
    nst = GPB * P
    eye = jnp.eye(GPB, dtype=bool)

    def expand(t):
        t = t.astype(F32).reshape(NGB, GPB, C, P)
        t = jnp.where(eye[None, :, None, :, None], t[:, :, :, None, :], 0.0)
        return t.reshape(NGB, GPB * C, nst)

    def lanes(t):
        return t.astype(F32).reshape(NGB, 1, nst)

    args = (lanes(lam_re), lanes(lam_im), lanes(jnp.broadcast_to(log_dt[:, None], (G, P))),
            expand(jnp.swapaxes(b_re, 1, 2)), expand(jnp.swapaxes(b_im, 1, 2)), expand(c_re), expand(c_im),
            d_skip.astype(F32).reshape(NGB, 1, LANES))
    row_spec = pl.BlockSpec((1, 1, nst), lambda gb: (gb, 0, 0))
    mat_spec = pl.BlockSpec((1, LANES, nst), lambda gb: (gb, 0, 0))
    n_pw = SEG + SUBLANES
    nsu = nst // NSUB
    return pl.pallas_call(
        _s5_prep_kernel,
        grid=(NGB,),
        in_specs=[row_spec, row_spec, row_spec, mat_spec, mat_spec, mat_spec, mat_spec,
                  pl.BlockSpec((1, 1, LANES), lambda gb: (gb, 0, 0))],
        out_specs=[
            pl.BlockSpec((1, NSUB, Q * SUBW, 2 * nsu), lambda gb: (gb, 0, 0, 0)),
            pl.BlockSpec((1, NSUB, Q * SUBW, 2 * nsu), lambda gb: (gb, 0, 0, 0)),
            pl.BlockSpec((1, NSUB, Q * SUBW, MXU_DIM), lambda gb: (gb, 0, 0, 0)),
            pl.BlockSpec((1, n_pw, 2 * nst), lambda gb: (gb, 0, 0)),
        ],
        out_shape=[
            jax.ShapeDtypeStruct((NGB, NSUB, Q * SUBW, 2 * nsu), BF16),
            jax.ShapeDtypeStruct((NGB, NSUB, Q * SUBW, 2 * nsu), BF16),
            jax.ShapeDtypeStruct((NGB, NSUB, Q * SUBW, MXU_DIM), BF16),
            jax.ShapeDtypeStruct((NGB, n_pw, 2 * nst), F32),
        ],
        scratch_shapes=[pltpu.VMEM((Q * LANES, 2 * nst), F32)],
        compiler_params=pltpu.CompilerParams(dimension_semantics=("arbitrary",), vmem_limit_bytes=VMEM_LIMIT),
        name="s5_prep",
    )(*args)


def _s5_layer(x, g, lam_re, lam_im, log_dt, b_re, b_im, c_re, c_im, d_skip, w_glu, b_glu, *, ma=64, mc=32):
    bsz, L, d = x.shape
    rows = bsz * L // Q
    rows_b = L // Q
    assert rows_b == NSEG * SEG and d == NGB * LANES
    w, vt, r_rev, pw = _s5_tables(lam_re, lam_im, log_dt, b_re, b_im, c_re, c_im, d_skip)
    xt = x.reshape(bsz * L, d)
    cp = pltpu.CompilerParams(dimension_semantics=("arbitrary",), vmem_limit_bytes=VMEM_LIMIT)

    h = pl.pallas_call(
        _s5_norm_kernel,
        grid=(rows // ma,),
        in_specs=[pl.BlockSpec((ma * Q, d), lambda i: (i, 0)), _const_spec((1, d))],
        out_specs=pl.BlockSpec((NGB, ma, Q * LANES), lambda i: (0, i, 0)),
        out_shape=jax.ShapeDtypeStruct((NGB, rows, Q * LANES), BF16),
        scratch_shapes=[pltpu.VMEM((NGB, ma * Q, LANES), F32)],
        compiler_params=cp,
        name="s5_norm",
    )(xt, g.reshape(1, d))

    nstate = pw.shape[-1]
    nslab = nstate // LANES
    y = pl.pallas_call(
        _s5_core_kernel,
        grid=(NGB, bsz),
        in_specs=[
            pl.BlockSpec((1, rows_b, Q * LANES), lambda gb, b: (gb, b, 0)),
            pl.BlockSpec((1,) + w.shape[1:], lambda gb, b: (gb, 0, 0, 0)),
            pl.BlockSpec((1,) + r_rev.shape[1:], lambda gb, b: (gb, 0, 0, 0)),
            pl.BlockSpec((1,) + vt.shape[1:], lambda gb, b: (gb, 0, 0, 0)),
            pl.BlockSpec((1, pw.shape[1], nstate), lambda gb, b: (gb, 0, 0)),
        ],
        out_specs=pl.BlockSpec((1, rows_b, Q * LANES), lambda gb, b: (gb, b, 0)),
        out_shape=jax.ShapeDtypeStruct((NGB, rows, Q * LANES), BF16),
        scratch_shapes=[
            pltpu.VMEM((nslab, NSEG * PITCH, LANES), F32),
            pltpu.VMEM((nslab, NSEG * PITCH, LANES), F32),
        ],
        compiler_params=pltpu.CompilerParams(
            dimension_semantics=("arbitrary", "arbitrary"), vmem_limit_bytes=VMEM_LIMIT),
        name="s5_core",
    )(h, w, r_rev, vt, pw)

    out = pl.pallas_call(
        _s5_out_kernel,
        grid=(rows // mc,),
        in_specs=[
            pl.BlockSpec((mc * Q, d), lambda i: (i, 0)),
            pl.BlockSpec((NGB, mc, Q * LANES), lambda i: (0, i, 0)),
            _const_spec((d, d)),
            _const_spec((1, d)),
        ],
        out_specs=pl.BlockSpec((mc * Q, d), lambda i: (i, 0)),
        out_shape=jax.ShapeDtypeStruct((bsz * L, d), F32),
        scratch_shapes=[pltpu.VMEM((NGB, mc * Q, LANES), F32)],
        compiler_params=cp,
        name="s5_out",
    )(xt, y, w_glu.astype(BF16), (0.5 * b_glu).reshape(1, d))
    return out.reshape(bsz, L, d)


def kernel(x, positions, norm_mix, norm_ffn, norm_final, s5_lambda_re, s5_lambda_im, s5_log_dt, s5_b_re, s5_b_im, s5_c_re, s5_c_im, s5_d, s5_w_glu, s5_b_glu, attn_w_qkv, attn_b_qkv, attn_sinks, attn_w_o, attn_b_o, ffn_w_up, ffn_w_conv, ffn_b_conv, ffn_w_down):
    depth = norm_mix.shape[0]
    w_up = ffn_w_up.astype(BF16)
    w_down = ffn_w_down.astype(BF16)
    for i in range(depth):
        j = i // 2
        if i % 2 == 0:
            x = _s5_layer(x, norm_mix[i], s5_lambda_re[j], s5_lambda_im[j], s5_log_dt[j],
                          s5_b_re[j], s5_b_im[j], s5_c_re[j], s5_c_im[j], s5_d[j], s5_w_glu[j], s5_b_glu[j])
        else:
            x = _swa_layer(x, positions, norm_mix[i], attn_w_qkv[j], attn_b_qkv[j], attn_sinks[j],
                           attn_w_o[j], attn_b_o[j])
        x = _conv_ffn(x, norm_ffn[i], w_up, ffn_w_conv[i], ffn_b_conv[i], w_down, i,
                      norm_final if i == depth - 1 else None)
    return x
```

```python
import functools

import numpy as np
import jax
import jax.numpy as jnp
from jax import lax
from jax.experimental import pallas as pl
from jax.experimental.pallas import tpu as pltpu

F32 = jnp.float32
BF16 = jnp.bfloat16

EPS = 1e-5
NEG_INF = -1e30
LOG2E = 1.4426950408889634
GELU_C0 = float(np.sqrt(2.0 / np.pi).astype(np.float32))

HEAD_DIM = 64
N_KV_HEADS = 4
ROPE_DIM = 16
ROPE_THETA = 500000.0
ATTN_BLOCK = 128
S5_GROUP = 16

LANES = 128
SUBLANES = 8
MXU_DIM = 256
VMEM_LIMIT = 56 * 1024 * 1024

Q = 16
NGB = 8
GPB = LANES // S5_GROUP
NSUB = 4
SUBW = LANES // NSUB
TPT = MXU_DIM // SUBW
SEG = 64
NSEG = 8
PITCH = 72


def _rms(x, g):
    ms = jnp.mean(x * x, axis=-1, keepdims=True)
    return x * lax.rsqrt(ms + EPS) * g


def _cmul(ar, ai, br, bi):
    return ar * br - ai * bi, ar * bi + ai * br


def _const_spec(shape):
    nd = len(shape)
    return pl.BlockSpec(shape, lambda *_: (0,) * nd, pipeline_mode=pl.Buffered(1))


def _ffn_kernel(x_ref, g_ref, wu_ref, wc_ref, bc_ref, wd_ref, gf_ref, o_ref, act_ref, carry_ref,
                *, fb, final_norm):
    tm = x_ref.shape[1]
    d_ff = wd_ref.shape[0]

    @pl.when(pl.program_id(1) == 0)
    def _():
        carry_ref[...] = jnp.zeros_like(carry_ref)

    x = x_ref[0]
    h = _rms(x, g_ref[...]).astype(BF16)
    rows = lax.broadcasted_iota(jnp.int32, (SUBLANES, fb), 0)

    def conv_block(col):
        u = jnp.dot(h, wu_ref[:, col:col + fb], preferred_element_type=F32)
        prev = carry_ref[:, col:col + fb]
        carry_ref[:, col:col + fb] = u[tm - SUBLANES:, :]
        s1 = pltpu.roll(u, 1, 0)
        s2 = pltpu.roll(u, 2, 0)
        t1 = jnp.where(rows < 1, pltpu.roll(prev, 1, 0), s1[:SUBLANES])
        t2 = jnp.where(rows < 2, pltpu.roll(prev, 2, 0), s2[:SUBLANES])
        s1 = jnp.concatenate([t1, s1[SUBLANES:]], axis=0)
        s2 = jnp.concatenate([t2, s2[SUBLANES:]], axis=0)
        w = wc_ref[:, col:col + fb]
        return w[0:1] * s2 + w[1:2] * s1 + w[2:3] * u + bc_ref[:, col:col + fb]

    for j in range(d_ff // fb):
        a = conv_block(j * fb)
        v = conv_block(d_ff + j * fb)
        act_ref[:, j * fb:(j + 1) * fb] = (a * (1.0 + jnp.tanh(a)) * v).astype(BF16)

    y = jnp.dot(act_ref[...], wd_ref[...], preferred_element_type=F32) + x
    if final_norm:
        y = _rms(y, gf_ref[...])
    o_ref[0] = y


def _layer_spec(shape, layer):
    nd = len(shape)
    return pl.BlockSpec((None,) + shape, lambda *_: (layer,) + (0,) * nd, pipeline_mode=pl.Buffered(1))


def _conv_ffn(x, g, w_up, w_conv, b_conv, w_down, layer, g_final, *, tm=1024, fb=256):
    bsz, L, d = x.shape
    d_ff = w_down.shape[1]
    final_norm = g_final is not None
    gf = g_final if final_norm else g
    kern = functools.partial(_ffn_kernel, fb=fb, final_norm=final_norm)
    half_gate = jnp.where(jnp.arange(2 * d_ff) < d_ff, 0.5, 1.0).astype(F32)
    w_conv = w_conv * half_gate[None, :]
    b_conv = b_conv * half_gate
    return pl.pallas_call(
        kern,
        grid=(bsz, L // tm),
        in_specs=[
            pl.BlockSpec((1, tm, d), lambda b, i: (b, i, 0)),
            _const_spec((1, d)),
            _layer_spec((d, 2 * d_ff), layer),
            _const_spec((3, 2 * d_ff)),
            _const_spec((1, 2 * d_ff)),
            _layer_spec((d_ff, d), layer),
            _const_spec((1, d)),
        ],
        out_specs=pl.BlockSpec((1, tm, d), lambda b, i: (b, i, 0)),
        out_shape=jax.ShapeDtypeStruct(x.shape, F32),
        scratch_shapes=[
            pltpu.VMEM((tm, d_ff), BF16),
            pltpu.VMEM((SUBLANES, 2 * d_ff), F32),
        ],
        compiler_params=pltpu.CompilerParams(
            dimension_semantics=("arbitrary", "arbitrary"), vmem_limit_bytes=VMEM_LIMIT),
        name="conv_ffn",
    )(x, g.reshape(1, d), w_up, w_conv, b_conv.reshape(1, -1), w_down, gf.reshape(1, d))


def _qkv_kernel(x_ref, pos_ref, g_ref, wt_ref, bt_ref, fr_ref, qt_ref, k_ref, vt_ref):
    d = x_ref.shape[2]
    nkv = N_KV_HEADS * HEAD_DIM
    half = ROPE_DIM // 2
    h = _rms(x_ref[0], g_ref[...]).astype(BF16)
    qkvt = lax.dot_general(wt_ref[...], h, (((1,), (1,)), ((), ())), preferred_element_type=F32) + bt_ref[...]
    ang = fr_ref[...] * pos_ref[0].astype(F32)
    cos = jnp.cos(ang)
    sin = jnp.sin(ang)

    def rotate(base):
        t1 = qkvt[base:base + half]
        t2 = qkvt[base + half:base + ROPE_DIM]
        return jnp.concatenate([t1 * cos - t2 * sin, t2 * cos + t1 * sin, qkvt[base + ROPE_DIM:base + HEAD_DIM]],
                               axis=0)

    for hq in range(d // HEAD_DIM):
        qt_ref[0, hq * HEAD_DIM:(hq + 1) * HEAD_DIM, :] = rotate(hq * HEAD_DIM).astype(BF16)
    kt = jnp.concatenate([rotate(d + hk * HEAD_DIM) for hk in range(N_KV_HEADS)], axis=0)
    k_ref[0] = kt.T.astype(BF16)
    vt_ref[0] = qkvt[d + nkv:].astype(BF16)


def _attn_kernel(sink_ref, qt_ref, kc_ref, kp_ref, vtc_ref, vtp_ref, x_ref, wo_ref, bo_ref, o_ref, ot_ref):
    tq = x_ref.shape[1]
    blk = ATTN_BLOCK
    q_per_kv = qt_ref.shape[1] // (N_KV_HEADS * HEAD_DIM)
    first_tile = pl.program_id(1) == 0
    key = lax.broadcasted_iota(jnp.int32, (2 * blk, q_per_kv * blk), 0)
    col = lax.broadcasted_iota(jnp.int32, (2 * blk, q_per_kv * blk), 1)
    qi = col & (blk - 1)
    valid = jnp.logical_or(jnp.logical_and(key < blk, key > qi), jnp.logical_and(key >= blk, key - blk <= qi))
    valid_first = jnp.logical_and(valid, jnp.logical_or(key >= blk, jnp.logical_not(first_tile)))
    bias = jnp.where(valid, 0.0, NEG_INF).astype(F32)
    bias_first = jnp.where(valid_first, 0.0, NEG_INF).astype(F32)
    col1 = lax.broadcasted_iota(jnp.int32, (1, q_per_kv * blk), 1)
    sinks = []
    for hk in range(N_KV_HEADS):
        sk = jnp.zeros(col1.shape, F32)
        for g in range(q_per_kv):
            sk = jnp.where(jnp.logical_and(col1 >= g * blk, col1 < (g + 1) * blk),
                           sink_ref[hk * q_per_kv + g] * LOG2E, sk)
        sinks.append(sk)

    def scores(n, hk):
        cols = slice(n * blk, (n + 1) * blk)
        if n == 0:
            k2 = jnp.concatenate([kp_ref[0], kc_ref[0, cols, :]], axis=0)
            mask = bias_first
        else:
            k2 = kc_ref[0, (n - 1) * blk:(n + 1) * blk, :]
            mask = bias
        heads = [hk * q_per_kv + g for g in range(q_per_kv)]
        kk = k2[:, hk * HEAD_DIM:(hk + 1) * HEAD_DIM]
        qc = jnp.concatenate([qt_ref[0, hq * HEAD_DIM:(hq + 1) * HEAD_DIM, cols] for hq in heads], axis=1)
        return jnp.dot(kk, qc, preferred_element_type=F32) + mask

    def finish(n, hk, s):
        cols = slice(n * blk, (n + 1) * blk)
        if n == 0:
            v2t = jnp.concatenate([vtp_ref[0], vtc_ref[0, :, cols]], axis=1)
        else:
            v2t = vtc_ref[0, :, (n - 1) * blk:(n + 1) * blk]
        heads = [hk * q_per_kv + g for g in range(q_per_kv)]
        vv = v2t[hk * HEAD_DIM:(hk + 1) * HEAD_DIM, :]
        m = jnp.maximum(jnp.max(s, axis=0, keepdims=True), sinks[hk])
        p = jnp.exp2(s - m)
        inv = 1.0 / (jnp.sum(p, axis=0, keepdims=True) + jnp.exp2(sinks[hk] - m))
        ot = jnp.dot(vv, p.astype(BF16), preferred_element_type=F32) * inv
        for g, hq in enumerate(heads):
            ot_ref[hq * HEAD_DIM:(hq + 1) * HEAD_DIM, cols] = ot[:, g * blk:(g + 1) * blk].astype(BF16)

    units = [(n, hk) for n in range(tq // blk) for hk in range(N_KV_HEADS)]
    s_cur = scores(*units[0])
    for i, unit in enumerate(units):
        s_next = scores(*units[i + 1]) if i + 1 < len(units) else None
        finish(*unit, s_cur)
        s_cur = s_next
    attn = lax.dot_general(ot_ref[...], wo_ref[...], (((0,), (0,)), ((), ())), preferred_element_type=F32)
    o_ref[0] = attn + bo_ref[...] + x_ref[0]


def _swa_layer(x, pos, g, w_qkv, b_qkv, sinks, w_o, b_o, *, tm=1024, tq=512):
    bsz, L, d = x.shape
    nkv = N_KV_HEADS * HEAD_DIM
    qkv_dim = w_qkv.shape[1]
    inv_freq = 1.0 / jnp.power(ROPE_THETA, jnp.arange(0, ROPE_DIM, 2, dtype=F32) / ROPE_DIM)
    scale = jnp.where(jnp.arange(qkv_dim) < d, LOG2E * HEAD_DIM ** -0.5, 1.0).astype(F32)
    wt = (w_qkv * scale[None, :]).T.astype(BF16)
    bt = (b_qkv * scale).astype(F32).reshape(qkv_dim, 1)

    qt, k, vt = pl.pallas_call(
        _qkv_kernel,
        grid=(bsz, L // tm),
        in_specs=[
            pl.BlockSpec((1, tm, d), lambda b, i: (b, i, 0)),
            pl.BlockSpec((1, 1, tm), lambda b, i: (b, 0, i)),
            _const_spec((1, d)),
            _const_spec((qkv_dim, d)),
            _const_spec((qkv_dim, 1)),
            _const_spec((ROPE_DIM // 2, 1)),
        ],
        out_specs=[
            pl.BlockSpec((1, d, tm), lambda b, i: (b, 0, i)),
            pl.BlockSpec((1, tm, nkv), lambda b, i: (b, i, 0)),
            pl.BlockSpec((1, nkv, tm), lambda b, i: (b, 0, i)),
        ],
        out_shape=[
            jax.ShapeDtypeStruct((bsz, d, L), BF16),
            jax.ShapeDtypeStruct((bsz, L, nkv), BF16),
            jax.ShapeDtypeStruct((bsz, nkv, L), BF16),
        ],
        compiler_params=pltpu.CompilerParams(
            dimension_semantics=("arbitrary", "arbitrary"), vmem_limit_bytes=VMEM_LIMIT),
        name="qkv_rope",
    )(x, pos.reshape(bsz, 1, L), g.reshape(1, d), wt, bt, inv_freq.reshape(-1, 1))

    bpt = tq // ATTN_BLOCK
    prev_block = lambda i: jnp.maximum(i * bpt - 1, 0)
    return pl.pallas_call(
        _attn_kernel,
        grid=(bsz, L // tq),
        in_specs=[
            pl.BlockSpec(memory_space=pltpu.SMEM),
            pl.BlockSpec((1, d, tq), lambda b, i: (b, 0, i)),
            pl.BlockSpec((1, tq, nkv), lambda b, i: (b, i, 0)),
            pl.BlockSpec((1, ATTN_BLOCK, nkv), lambda b, i: (b, prev_block(i), 0)),
            pl.BlockSpec((1, nkv, tq), lambda b, i: (b, 0, i)),
            pl.BlockSpec((1, nkv, ATTN_BLOCK), lambda b, i: (b, 0, prev_block(i))),
            pl.BlockSpec((1, tq, d), lambda b, i: (b, i, 0)),
            _const_spec((d, d)),
            _const_spec((1, d)),
        ],
        out_specs=pl.BlockSpec((1, tq, d), lambda b, i: (b, i, 0)),
        out_shape=jax.ShapeDtypeStruct(x.shape, F32),
        scratch_shapes=[pltpu.VMEM((d, tq), BF16)],
        compiler_params=pltpu.CompilerParams(
            dimension_semantics=("arbitrary", "arbitrary"), vmem_limit_bytes=VMEM_LIMIT),
        name="swa_attn",
    )(sinks.astype(F32), qt, k, k, vt, vt, x, w_o.astype(BF16), b_o.reshape(1, d))


def _s5_norm_kernel(x_ref, g_ref, h_ref, hs_ref):
    ma = h_ref.shape[1]
    hs = _rms(x_ref[...], g_ref[...])
    for gb in range(NGB):
        hs_ref[gb] = hs[:, gb * LANES:(gb + 1) * LANES]
    lane = lax.broadcasted_iota(jnp.int32, (ma, LANES), 1)
    for gb in range(NGB):
        for i in range(Q // NSUB):
            tok = [hs_ref[gb, pl.ds(NSUB * i + r, ma, stride=Q), :] for r in range(NSUB)]
            for u in range(NSUB):
                acc = None
                for r in range(NSUB):
                    t = tok[r] if r == u else pltpu.roll(tok[r], ((r - u) * SUBW) % LANES, 1)
                    acc = t if acc is None else jnp.where(
                        jnp.logical_and(lane >= r * SUBW, lane < (r + 1) * SUBW), t, acc)
                col = u * Q * SUBW + i * LANES
                h_ref[gb, :, col:col + LANES] = acc.astype(BF16)


def _s5_core_kernel(h_ref, w_ref, r_ref, vt_ref, pw_ref, y_ref, z_ref, sx_ref):
    nslab = z_ref.shape[0]
    nc = nslab // 2
    kw = Q * SUBW
    nst = w_ref.shape[3] // 2
    spu = nst // LANES
    lhs = [h_ref[0, :, u * kw:(u + 1) * kw] for u in range(NSUB)]
    for u in range(NSUB):
        z = jnp.dot(lhs[u], w_ref[0, u], preferred_element_type=F32)
        for j in range(NSEG):
            for c in range(2 * spu):
                slab = (nc if c >= spu else 0) + u * spu + c % spu
                z_ref[slab, j * PITCH:j * PITCH + SEG, :] = z[j * SEG:(j + 1) * SEG, c * LANES:(c + 1) * LANES]

    half = nc * LANES

    def table(row0, nrows, c, imag):
        lo = (half if imag else 0) + c * LANES
        return pw_ref[0, row0:row0 + nrows, lo:lo + LANES]

    a_re = [jnp.broadcast_to(table(1, 1, c, False), (NSEG, LANES)) for c in range(nc)]
    a_im = [jnp.broadcast_to(table(1, 1, c, True), (NSEG, LANES)) for c in range(nc)]

    def scan_step(t, state):
        new_re, new_im = [], []
        for c in range(nc):
            s_re, s_im = state[c], state[c + nc]
            sx_ref[c, pl.ds(t, NSEG, stride=PITCH), :] = s_re
            sx_ref[c + nc, pl.ds(t, NSEG, stride=PITCH), :] = s_im
            z_re = z_ref[c, pl.ds(t, NSEG, stride=PITCH), :]
            z_im = z_ref[c + nc, pl.ds(t, NSEG, stride=PITCH), :]
            new_re.append(a_re[c] * s_re - a_im[c] * s_im + z_re)
            new_im.append(a_re[c] * s_im + a_im[c] * s_re + z_im)
        return tuple(new_re + new_im)

    zero = jnp.zeros((NSEG, LANES), F32)
    end = lax.fori_loop(0, SEG, scan_step, (zero,) * nslab)

    for c in range(nc):
        ap_re = table(SEG, 1, c, False)
        ap_im = table(SEG, 1, c, True)
        c_re = jnp.zeros((1, LANES), F32)
        c_im = jnp.zeros((1, LANES), F32)
        for j in range(1, NSEG):
            e_re = end[c][j - 1:j]
            e_im = end[c + nc][j - 1:j]
            c_re, c_im = (e_re + ap_re * c_re - ap_im * c_im, e_im + ap_re * c_im + ap_im * c_re)
            p_re = table(0, SEG, c, False)
            p_im = table(0, SEG, c, True)
            rows = slice(j * PITCH, j * PITCH + SEG)
            sx_ref[c, rows, :] = sx_ref[c, rows, :] + (p_re * c_re - p_im * c_im)
            sx_ref[c + nc, rows, :] = sx_ref[c + nc, rows, :] + (p_re * c_im + p_im * c_re)

    nt = kw // MXU_DIM
    for u in range(NSUB):
        slabs = [u * spu + c for c in range(spu)] + [nc + u * spu + c for c in range(spu)]
        sx = jnp.concatenate(
            [jnp.concatenate([sx_ref[c, j * PITCH:j * PITCH + SEG, :] for c in slabs], axis=1)
             for j in range(NSEG)], axis=0).astype(BF16)
        for b in range(nt):
            acc = jnp.dot(lhs[u][:, :(b + 1) * MXU_DIM], r_ref[0, u, (nt - 1 - b) * MXU_DIM:, :],
                          preferred_element_type=F32)
            acc = acc + lax.dot_general(sx, vt_ref[0, u, b * MXU_DIM:(b + 1) * MXU_DIM, :],
                                        (((1,), (1,)), ((), ())), preferred_element_type=F32)
            y_ref[0, :, u * kw + b * MXU_DIM:u * kw + (b + 1) * MXU_DIM] = acc.astype(BF16)


def _s5_out_kernel(x_ref, y_ref, w_ref, b_ref, o_ref, g_ref):
    mc = y_ref.shape[1]
    c0 = GELU_C0
    c1 = GELU_C0 * 0.044715
    lane = lax.broadcasted_iota(jnp.int32, (mc, LANES), 1)
    for gb in range(NGB):
        for i in range(Q // NSUB):
            sub = [y_ref[gb, :, u * Q * SUBW + i * LANES:u * Q * SUBW + (i + 1) * LANES].astype(F32)
                   for u in range(NSUB)]
            for r in range(NSUB):
                y = None
                for u in range(NSUB):
                    t = sub[u] if u == r else pltpu.roll(sub[u], ((u - r) * SUBW) % LANES, 1)
                    y = t if y is None else jnp.where(
                        jnp.logical_and(lane >= u * SUBW, lane < (u + 1) * SUBW), t, y)
                inner = y * (c0 + c1 * (y * y))
                g_ref[gb, pl.ds(NSUB * i + r, mc, stride=Q), :] = (0.25 * y) * (1.0 + jnp.tanh(inner))
    gh = jnp.concatenate([g_ref[gb] for gb in range(NGB)], axis=1)
    half_gate = jnp.dot(gh.astype(BF16), w_ref[...], preferred_element_type=F32) + b_ref[...]
    o_ref[...] = x_ref[...] + gh * (1.0 + jnp.tanh(half_gate))


def _s5_prep_kernel(lr_ref, li_ref, ldt_ref, be_re_ref, be_im_ref, ce_re_ref, ce_im_ref, d_ref,
                    w_ref, vt_ref, r_ref, pw_ref, wf_ref):
    nst = lr_ref.shape[2]
    nsu = nst // NSUB
    lr, li = lr_ref[0], li_ref[0]
    dt = jnp.exp(ldt_ref[0])
    th_re, th_im = lr * dt, li * dt

    def apow(n):
        mag = jnp.exp(n * th_re)
        return mag * jnp.cos(n * th_im), mag * jnp.sin(n * th_im)

    n_small = lax.broadcasted_iota(jnp.int32, (3 * SUBLANES, 1), 0).astype(F32)
    p_re, p_im = apow(n_small)
    a_re, a_im = p_re[1:2], p_im[1:2]
    den = lr * lr + li * li
    z_re = ((a_re - 1.0) * lr + a_im * li) / den
    z_im = (a_im * lr - (a_re - 1.0) * li) / den
    bb_re, bb_im = _cmul(z_re, z_im, be_re_ref[0], be_im_ref[0])
    ce_re, ce_im = ce_re_ref[0], ce_im_ref[0]

    for s in range(Q):
        rows = slice(s * LANES, (s + 1) * LANES)
        n = Q - 1 - s
        w_re, w_im = _cmul(bb_re, bb_im, p_re[n:n + 1], p_im[n:n + 1])
        wf_ref[rows, :nst] = w_re
        wf_ref[rows, nst:] = w_im
        v_re, v_im = _cmul(ce_re, ce_im, p_re[s + 1:s + 2], p_im[s + 1:s + 2])
        for u in range(NSUB):
            src_rows = slice(u * SUBW, (u + 1) * SUBW)
            src_cols = slice(u * nsu, (u + 1) * nsu)
            dst_rows = slice(s * SUBW, (s + 1) * SUBW)
            w_ref[0, u, dst_rows, :nsu] = w_re[src_rows, src_cols].astype(BF16)
            w_ref[0, u, dst_rows, nsu:] = w_im[src_rows, src_cols].astype(BF16)
            vt_ref[0, u, dst_rows, :nsu] = v_re[src_rows, src_cols].astype(BF16)
            vt_ref[0, u, dst_rows, nsu:] = (-v_im[src_rows, src_cols]).astype(BF16)

    def split(a):
        hi = a.astype(BF16)
        return hi, (a - hi.astype(F32)).astype(BF16)

    def dot_t(a, b):
        return lax.dot_general(a, b, (((1,), (1,)), ((), ())), preferred_element_type=F32)

    w_hi, w_lo = split(wf_ref[...])
    c_hi, c_lo = split(jnp.concatenate([ce_re, -ce_im], axis=1))
    kall = dot_t(w_hi, c_hi) + dot_t(w_lo, c_hi) + dot_t(w_hi, c_lo)
    ri = lax.broadcasted_iota(jnp.int32, (LANES, LANES), 0)
    ci = lax.broadcasted_iota(jnp.int32, (LANES, LANES), 1)
    lag0 = kall[(Q - 1) * LANES:] + jnp.where(ri == ci, d_ref[0], 0.0)
    kall = jnp.concatenate([kall[:(Q - 1) * LANES], lag0], axis=0)

    rolled = [kall] + [pltpu.roll(kall, dd * SUBW, 1) for dd in range(1, NSUB)]
    lane = lax.broadcasted_iota(jnp.int32, (Q * SUBW, LANES), 1)
    for u in range(NSUB):
        sub_rows = [jnp.concatenate([x[s * LANES + u * SUBW:s * LANES + (u + 1) * SUBW] for s in range(Q)], axis=0)
                    for x in rolled]
        for piece in range(MXU_DIM // LANES):
            acc = None
            for r in range(NSUB):
                up = (TPT - 1 - (piece * NSUB + r)) * SUBW
                x = sub_rows[(r - u) % NSUB]
                if up:
                    x = jnp.concatenate([x[up:], jnp.zeros((up, LANES), F32)], axis=0)
                acc = x if acc is None else jnp.where(
                    jnp.logical_and(lane >= r * SUBW, lane < (r + 1) * SUBW), x, acc)
            r_ref[0, u, :, piece * LANES:(piece + 1) * LANES] = acc.astype(BF16)

    n_big = float(Q) * lax.broadcasted_iota(jnp.int32, (pw_ref.shape[1], 1), 0).astype(F32)
    t_re, t_im = apow(n_big)
    pw_ref[0, :, :nst] = t_re
    pw_ref[0, :, nst:] = t_im


def _s5_tables(lam_re, lam_im, log_dt, b_re, b_im, c_re, c_im, d_skip):
    G, P = lam_re.shape
    C = b_re.shape[-1]
    nst = GPB * P
    eye = jnp.eye(GPB, dtype=bool)

    def expand(t):
        t = t.astype(F32).reshape(NGB, GPB, C, P)
        t = jnp.where(eye[None, :, None, :, None], t[:, :, :, None, :], 0.0)
        return t.reshape(NGB, GPB * C, nst)

    def lanes(t):
        return t.astype(F32).reshape(NGB, 1, nst)

    args = (lanes(lam_re), lanes(lam_im), lanes(jnp.broadcast_to(log_dt[:, None], (G, P))),
            expand(jnp.swapaxes(b_re, 1, 2)), expand(jnp.swapaxes(b_im, 1, 2)), expand(c_re), expand(c_im),
            d_skip.astype(F32).reshape(NGB, 1, LANES))
    row_spec = pl.BlockSpec((1, 1, nst), lambda gb: (gb, 0, 0))
    mat_spec = pl.BlockSpec((1, LANES, nst), lambda gb: (gb, 0, 0))
    n_pw = SEG + SUBLANES
    nsu = nst // NSUB
    return pl.pallas_call(
        _s5_prep_kernel,
        grid=(NGB,),
        in_specs=[row_spec, row_spec, row_spec, mat_spec, mat_spec, mat_spec, mat_spec,
                  pl.BlockSpec((1, 1, LANES), lambda gb: (gb, 0, 0))],
        out_specs=[
            pl.BlockSpec((1, NSUB, Q * SUBW, 2 * nsu), lambda gb: (gb, 0, 0, 0)),
            pl.BlockSpec((1, NSUB, Q * SUBW, 2 * nsu), lambda gb: (gb, 0, 0, 0)),
            pl.BlockSpec((1, NSUB, Q * SUBW, MXU_DIM), lambda gb: (gb, 0, 0, 0)),
            pl.BlockSpec((1, n_pw, 2 * nst), lambda gb: (gb, 0, 0)),
        ],
        out_shape=[
            jax.ShapeDtypeStruct((NGB, NSUB, Q * SUBW, 2 * nsu), BF16),
            jax.ShapeDtypeStruct((NGB, NSUB, Q * SUBW, 2 * nsu), BF16),
            jax.ShapeDtypeStruct((NGB, NSUB, Q * SUBW, MXU_DIM), BF16),
            jax.ShapeDtypeStruct((NGB, n_pw, 2 * nst), F32),
        ],
        scratch_shapes=[pltpu.VMEM((Q * LANES, 2 * nst), F32)],
        compiler_params=pltpu.CompilerParams(dimension_semantics=("arbitrary",), vmem_limit_bytes=VMEM_LIMIT),
        name="s5_prep",
    )(*args)


def _s5_layer(x, g, lam_re, lam_im, log_dt, b_re, b_im, c_re, c_im, d_skip, w_glu, b_glu, *, ma=64, mc=64):
    bsz, L, d = x.shape
    rows = bsz * L // Q
    rows_b = L // Q
    assert rows_b == NSEG * SEG and d == NGB * LANES
    w, vt, r_rev, pw = _s5_tables(lam_re, lam_im, log_dt, b_re, b_im, c_re, c_im, d_skip)
    xt = x.reshape(bsz * L, d)
    cp = pltpu.CompilerParams(dimension_semantics=("arbitrary",), vmem_limit_bytes=VMEM_LIMIT)

    h = pl.pallas_call(
        _s5_norm_kernel,
        grid=(rows // ma,),
        in_specs=[pl.BlockSpec((ma * Q, d), lambda i: (i, 0)), _const_spec((1, d))],
        out_specs=pl.BlockSpec((NGB, ma, Q * LANES), lambda i: (0, i, 0)),
        out_shape=jax.ShapeDtypeStruct((NGB, rows, Q * LANES), BF16),
        scratch_shapes=[pltpu.VMEM((NGB, ma * Q, LANES), F32)],
        compiler_params=cp,
        name="s5_norm",
    )(xt, g.reshape(1, d))

    nstate = pw.shape[-1]
    nslab = nstate // LANES
    y = pl.pallas_call(
        _s5_core_kernel,
        grid=(NGB, bsz),
        in_specs=[
            pl.BlockSpec((1, rows_b, Q * LANES), lambda gb, b: (gb, b, 0)),
            pl.BlockSpec((1,) + w.shape[1:], lambda gb, b: (gb, 0, 0, 0)),
            pl.BlockSpec((1,) + r_rev.shape[1:], lambda gb, b: (gb, 0, 0, 0)),
            pl.BlockSpec((1,) + vt.shape[1:], lambda gb, b: (gb, 0, 0, 0)),
            pl.BlockSpec((1, pw.shape[1], nstate), lambda gb, b: (gb, 0, 0)),
        ],
        out_specs=pl.BlockSpec((1, rows_b, Q * LANES), lambda gb, b: (gb, b, 0)),
        out_shape=jax.ShapeDtypeStruct((NGB, rows, Q * LANES), BF16),
        scratch_shapes=[
            pltpu.VMEM((nslab, NSEG * PITCH, LANES), F32),
            pltpu.VMEM((nslab, NSEG * PITCH, LANES), F32),
        ],
        compiler_params=pltpu.CompilerParams(
            dimension_semantics=("arbitrary", "arbitrary"), vmem_limit_bytes=VMEM_LIMIT),
        name="s5_core",
    )(h, w, r_rev, vt, pw)

    out = pl.pallas_call(
        _s5_out_kernel,
        grid=(rows // mc,),
        in_specs=[
            pl.BlockSpec((mc * Q, d), lambda i: (i, 0)),
            pl.BlockSpec((NGB, mc, Q * LANES), lambda i: (0, i, 0)),
            _const_spec((d, d)),
            _const_spec((1, d)),
        ],
        out_specs=pl.BlockSpec((mc * Q, d), lambda i: (i, 0)),
        out_shape=jax.ShapeDtypeStruct((bsz * L, d), F32),
        scratch_shapes=[pltpu.VMEM((NGB, mc * Q, LANES), F32)],
        compiler_params=cp,
        name="s5_out",
    )(xt, y, w_glu.astype(BF16), (0.5 * b_glu).reshape(1, d))
    return out.reshape(bsz, L, d)


def kernel(x, positions, norm_mix, norm_ffn, norm_final, s5_lambda_re, s5_lambda_im, s5_log_dt, s5_b_re, s5_b_im, s5_c_re, s5_c_im, s5_d, s5_w_glu, s5_b_glu, attn_w_qkv, attn_b_qkv, attn_sinks, attn_w_o, attn_b_o, ffn_w_up, ffn_w_conv, ffn_b_conv, ffn_w_down):
    depth = norm_mix.shape[0]
    w_up = ffn_w_up.astype(BF16)
    w_down = ffn_w_down.astype(BF16)
    for i in range(depth):
        j = i // 2
        if i % 2 == 0:
            x = _s5_layer(x, norm_mix[i], s5_lambda_re[j], s5_lambda_im[j], s5_log_dt[j],
                          s5_b_re[j], s5_b_im[j], s5_c_re[j], s5_c_im[j], s5_d[j], s5_w_glu[j], s5_b_glu[j])
        else:
            x = _swa_layer(x, positions, norm_mix[i], attn_w_qkv[j], attn_b_qkv[j], attn_sinks[j],
                           attn_w_o[j], attn_b_o[j])
        x = _conv_ffn(x, norm_ffn[i], w_up, ffn_w_conv[i], ffn_b_conv[i], w_down, i,
                      norm_final if i == depth - 1 else None)
    return x
```

```python
import functools

import numpy as np
import jax
import jax.numpy as jnp
from jax import lax
from jax.experimental import pallas as pl
from jax.experimental.pallas import tpu as pltpu

F32 = jnp.float32
BF16 = jnp.bfloat16

EPS = 1e-5
NEG_INF = -1e30
LOG2E = 1.4426950408889634
GELU_C0 = float(np.sqrt(2.0 / np.pi).astype(np.float32))

HEAD_DIM = 64
N_KV_HEADS = 4
ROPE_DIM = 16
ROPE_THETA = 500000.0
ATTN_BLOCK = 128
S5_GROUP = 16

LANES = 128
SUBLANES = 8
MXU_DIM = 256
VMEM_LIMIT = 56 * 1024 * 1024

Q = 16
NGB = 8
GPB = LANES // S5_GROUP
NSUB = 4
SUBW = LANES // NSUB
TPT = MXU_DIM // SUBW
SEG = 32
NSEG = 16
PITCH = 40
TOK_PITCH = 24


def _rms(x, g):
    ms = jnp.mean(x * x, axis=-1, keepdims=True)
    return x * lax.rsqrt(ms + EPS) * g


def _cmul(ar, ai, br, bi):
    return ar * br - ai * bi, ar * bi + ai * br


def _const_spec(shape):
    nd = len(shape)
    return pl.BlockSpec(shape, lambda *_: (0,) * nd, pipeline_mode=pl.Buffered(1))


def _ffn_kernel(x_ref, g_ref, wu_ref, wc_ref, bc_ref, wd_ref, gf_ref, o_ref, act_ref, carry_ref,
                *, fb, final_norm):
    tm = x_ref.shape[1]
    d_ff = wd_ref.shape[0]

    @pl.when(pl.program_id(1) == 0)
    def _():
        carry_ref[...] = jnp.zeros_like(carry_ref)

    x = x_ref[0]
    h = _rms(x, g_ref[...]).astype(BF16)
    rows = lax.broadcasted_iota(jnp.int32, (SUBLANES, fb), 0)

    def conv_block(col):
        u = jnp.dot(h, wu_ref[:, col:col + fb], preferred_element_type=F32)
        prev = carry_ref[:, col:col + fb]
        carry_ref[:, col:col + fb] = u[tm - SUBLANES:, :]
        s1 = pltpu.roll(u, 1, 0)
        s2 = pltpu.roll(u, 2, 0)
        t1 = jnp.where(rows < 1, pltpu.roll(prev, 1, 0), s1[:SUBLANES])
        t2 = jnp.where(rows < 2, pltpu.roll(prev, 2, 0), s2[:SUBLANES])
        s1 = jnp.concatenate([t1, s1[SUBLANES:]], axis=0)
        s2 = jnp.concatenate([t2, s2[SUBLANES:]], axis=0)
        w = wc_ref[:, col:col + fb]
        return w[0:1] * s2 + w[1:2] * s1 + w[2:3] * u + bc_ref[:, col:col + fb]

    for j in range(d_ff // fb):
        a = conv_block(j * fb)
        v = conv_block(d_ff + j * fb)
        act_ref[:, j * fb:(j + 1) * fb] = (a * (1.0 + jnp.tanh(a)) * v).astype(BF16)

    y = jnp.dot(act_ref[...], wd_ref[...], preferred_element_type=F32) + x
    if final_norm:
        y = _rms(y, gf_ref[...])
    o_ref[0] = y


def _layer_spec(shape, layer):
    nd = len(shape)
    return pl.BlockSpec((None,) + shape, lambda *_: (layer,) + (0,) * nd, pipeline_mode=pl.Buffered(1))


def _conv_ffn(x, g, w_up, w_conv, b_conv, w_down, layer, g_final, *, tm=1024, fb=256):
    bsz, L, d = x.shape
    d_ff = w_down.shape[1]
    final_norm = g_final is not None
    gf = g_final if final_norm else g
    kern = functools.partial(_ffn_kernel, fb=fb, final_norm=final_norm)
    half_gate = jnp.where(jnp.arange(2 * d_ff) < d_ff, 0.5, 1.0).astype(F32)
    w_conv = w_conv * half_gate[None, :]
    b_conv = b_conv * half_gate
    return pl.pallas_call(
        kern,
        grid=(bsz, L // tm),
        in_specs=[
            pl.BlockSpec((1, tm, d), lambda b, i: (b, i, 0)),
            _const_spec((1, d)),
            _layer_spec((d, 2 * d_ff), layer),
            _const_spec((3, 2 * d_ff)),
            _const_spec((1, 2 * d_ff)),
            _layer_spec((d_ff, d), layer),
            _const_spec((1, d)),
        ],
        out_specs=pl.BlockSpec((1, tm, d), lambda b, i: (b, i, 0)),
        out_shape=jax.ShapeDtypeStruct(x.shape, F32),
        scratch_shapes=[
            pltpu.VMEM((tm, d_ff), BF16),
            pltpu.VMEM((SUBLANES, 2 * d_ff), F32),
        ],
        compiler_params=pltpu.CompilerParams(
            dimension_semantics=("arbitrary", "arbitrary"), vmem_limit_bytes=VMEM_LIMIT),
        name="conv_ffn",
    )(x, g.reshape(1, d), w_up, w_conv, b_conv.reshape(1, -1), w_down, gf.reshape(1, d))


def _qkv_kernel(x_ref, pos_ref, g_ref, wt_ref, bt_ref, fr_ref, qt_ref, k_ref, vt_ref):
    d = x_ref.shape[2]
    nkv = N_KV_HEADS * HEAD_DIM
    half = ROPE_DIM // 2
    h = _rms(x_ref[0], g_ref[...]).astype(BF16)
    qkvt = lax.dot_general(wt_ref[...], h, (((1,), (1,)), ((), ())), preferred_element_type=F32) + bt_ref[...]
    ang = fr_ref[...] * pos_ref[0].astype(F32)
    cos = jnp.cos(ang)
    sin = jnp.sin(ang)

    def rotate(base):
        t1 = qkvt[base:base + half]
        t2 = qkvt[base + half:base + ROPE_DIM]
        return jnp.concatenate([t1 * cos - t2 * sin, t2 * cos + t1 * sin, qkvt[base + ROPE_DIM:base + HEAD_DIM]],
                               axis=0)

    for hq in range(d // HEAD_DIM):
        qt_ref[0, hq * HEAD_DIM:(hq + 1) * HEAD_DIM, :] = rotate(hq * HEAD_DIM).astype(BF16)
    kt = jnp.concatenate([rotate(d + hk * HEAD_DIM) for hk in range(N_KV_HEADS)], axis=0)
    k_ref[0] = kt.T.astype(BF16)
    vt_ref[0] = qkvt[d + nkv:].astype(BF16)


def _attn_kernel(sink_ref, qt_ref, kc_ref, kp_ref, vtc_ref, vtp_ref, x_ref, wo_ref, bo_ref, o_ref, ot_ref):
    tq = x_ref.shape[1]
    blk = ATTN_BLOCK
    q_per_kv = qt_ref.shape[1] // (N_KV_HEADS * HEAD_DIM)
    first_tile = pl.program_id(1) == 0
    key = lax.broadcasted_iota(jnp.int32, (2 * blk, q_per_kv * blk), 0)
    col = lax.broadcasted_iota(jnp.int32, (2 * blk, q_per_kv * blk), 1)
    qi = col & (blk - 1)
    valid = jnp.logical_or(jnp.logical_and(key < blk, key > qi), jnp.logical_and(key >= blk, key - blk <= qi))
    valid_first = jnp.logical_and(valid, jnp.logical_or(key >= blk, jnp.logical_not(first_tile)))
    bias = jnp.where(valid, 0.0, NEG_INF).astype(F32)
    bias_first = jnp.where(valid_first, 0.0, NEG_INF).astype(F32)
    col1 = lax.broadcasted_iota(jnp.int32, (1, q_per_kv * blk), 1)
    sinks = []
    for hk in range(N_KV_HEADS):
        sk = jnp.zeros(col1.shape, F32)
        for g in range(q_per_kv):
            sk = jnp.where(jnp.logical_and(col1 >= g * blk, col1 < (g + 1) * blk),
                           sink_ref[hk * q_per_kv + g] * LOG2E, sk)
        sinks.append(sk)

    def scores(n, hk):
        cols = slice(n * blk, (n + 1) * blk)
        if n == 0:
            k2 = jnp.concatenate([kp_ref[0], kc_ref[0, cols, :]], axis=0)
            mask = bias_first
        else:
            k2 = kc_ref[0, (n - 1) * blk:(n + 1) * blk, :]
            mask = bias
        heads = [hk * q_per_kv + g for g in range(q_per_kv)]
        kk = k2[:, hk * HEAD_DIM:(hk + 1) * HEAD_DIM]
        qc = jnp.concatenate([qt_ref[0, hq * HEAD_DIM:(hq + 1) * HEAD_DIM, cols] for hq in heads], axis=1)
        return jnp.dot(kk, qc, preferred_element_type=F32) + mask

    def finish(n, hk, s):
        cols = slice(n * blk, (n + 1) * blk)
        if n == 0:
            v2t = jnp.concatenate([vtp_ref[0], vtc_ref[0, :, cols]], axis=1)
        else:
            v2t = vtc_ref[0, :, (n - 1) * blk:(n + 1) * blk]
        heads = [hk * q_per_kv + g for g in range(q_per_kv)]
        vv = v2t[hk * HEAD_DIM:(hk + 1) * HEAD_DIM, :]
        m = jnp.maximum(jnp.max(s, axis=0, keepdims=True), sinks[hk])
        p = jnp.exp2(s - m)
        inv = 1.0 / (jnp.sum(p, axis=0, keepdims=True) + jnp.exp2(sinks[hk] - m))
        ot = jnp.dot(vv, p.astype(BF16), preferred_element_type=F32) * inv
        for g, hq in enumerate(heads):
            ot_ref[hq * HEAD_DIM:(hq + 1) * HEAD_DIM, cols] = ot[:, g * blk:(g + 1) * blk].astype(BF16)

    units = [(n, hk) for n in range(tq // blk) for hk in range(N_KV_HEADS)]
    s_cur = scores(*units[0])
    for i, unit in enumerate(units):
        s_next = scores(*units[i + 1]) if i + 1 < len(units) else None
        finish(*unit, s_cur)
        s_cur = s_next
    attn = lax.dot_general(ot_ref[...], wo_ref[...], (((0,), (0,)), ((), ())), preferred_element_type=F32)
    o_ref[0] = attn + bo_ref[...] + x_ref[0]


def _swa_layer(x, pos, g, w_qkv, b_qkv, sinks, w_o, b_o, *, tm=1024, tq=512):
    bsz, L, d = x.shape
    nkv = N_KV_HEADS * HEAD_DIM
    qkv_dim = w_qkv.shape[1]
    inv_freq = 1.0 / jnp.power(ROPE_THETA, jnp.arange(0, ROPE_DIM, 2, dtype=F32) / ROPE_DIM)
    scale = jnp.where(jnp.arange(qkv_dim) < d, LOG2E * HEAD_DIM ** -0.5, 1.0).astype(F32)
    wt = (w_qkv * scale[None, :]).T.astype(BF16)
    bt = (b_qkv * scale).astype(F32).reshape(qkv_dim, 1)

    qt, k, vt = pl.pallas_call(
        _qkv_kernel,
        grid=(bsz, L // tm),
        in_specs=[
            pl.BlockSpec((1, tm, d), lambda b, i: (b, i, 0)),
            pl.BlockSpec((1, 1, tm), lambda b, i: (b, 0, i)),
            _const_spec((1, d)),
            _const_spec((qkv_dim, d)),
            _const_spec((qkv_dim, 1)),
            _const_spec((ROPE_DIM // 2, 1)),
        ],
        out_specs=[
            pl.BlockSpec((1, d, tm), lambda b, i: (b, 0, i)),
            pl.BlockSpec((1, tm, nkv), lambda b, i: (b, i, 0)),
            pl.BlockSpec((1, nkv, tm), lambda b, i: (b, 0, i)),
        ],
        out_shape=[
            jax.ShapeDtypeStruct((bsz, d, L), BF16),
            jax.ShapeDtypeStruct((bsz, L, nkv), BF16),
            jax.ShapeDtypeStruct((bsz, nkv, L), BF16),
        ],
        compiler_params=pltpu.CompilerParams(
            dimension_semantics=("arbitrary", "arbitrary"), vmem_limit_bytes=VMEM_LIMIT),
        name="qkv_rope",
    )(x, pos.reshape(bsz, 1, L), g.reshape(1, d), wt, bt, inv_freq.reshape(-1, 1))

    bpt = tq // ATTN_BLOCK
    prev_block = lambda i: jnp.maximum(i * bpt - 1, 0)
    return pl.pallas_call(
        _attn_kernel,
        grid=(bsz, L // tq),
        in_specs=[
            pl.BlockSpec(memory_space=pltpu.SMEM),
            pl.BlockSpec((1, d, tq), lambda b, i: (b, 0, i)),
            pl.BlockSpec((1, tq, nkv), lambda b, i: (b, i, 0)),
            pl.BlockSpec((1, ATTN_BLOCK, nkv), lambda b, i: (b, prev_block(i), 0)),
            pl.BlockSpec((1, nkv, tq), lambda b, i: (b, 0, i)),
            pl.BlockSpec((1, nkv, ATTN_BLOCK), lambda b, i: (b, 0, prev_block(i))),
            pl.BlockSpec((1, tq, d), lambda b, i: (b, i, 0)),
            _const_spec((d, d)),
            _const_spec((1, d)),
        ],
        out_specs=pl.BlockSpec((1, tq, d), lambda b, i: (b, i, 0)),
        out_shape=jax.ShapeDtypeStruct(x.shape, F32),
        scratch_shapes=[pltpu.VMEM((d, tq), BF16)],
        compiler_params=pltpu.CompilerParams(
            dimension_semantics=("arbitrary", "arbitrary"), vmem_limit_bytes=VMEM_LIMIT),
        name="swa_attn",
    )(sinks.astype(F32), qt, k, k, vt, vt, x, w_o.astype(BF16), b_o.reshape(1, d))


def _s5_norm_kernel(x_ref, g_ref, h_ref, hs_ref):
    ma = h_ref.shape[1]
    hs = _rms(x_ref[...], g_ref[...])
    for gb in range(NGB):
        for k in range(ma):
            hs_ref[gb, k * TOK_PITCH:k * TOK_PITCH + Q, :] = hs[k * Q:(k + 1) * Q, gb * LANES:(gb + 1) * LANES]
    lane = lax.broadcasted_iota(jnp.int32, (ma, LANES), 1)
    for gb in range(NGB):
        for i in range(Q // NSUB):
            tok = [hs_ref[gb, pl.ds(NSUB * i + r, ma, stride=TOK_PITCH), :] for r in range(NSUB)]
            for u in range(NSUB):
                acc = None
                for r in range(NSUB):
                    t = tok[r] if r == u else pltpu.roll(tok[r], ((r - u) * SUBW) % LANES, 1)
                    acc = t if acc is None else jnp.where(
                        jnp.logical_and(lane >= r * SUBW, lane < (r + 1) * SUBW), t, acc)
                col = u * Q * SUBW + i * LANES
                h_ref[gb, :, col:col + LANES] = acc.astype(BF16)


def _s5_core_kernel(h_ref, w_ref, r_ref, vt_ref, pw_ref, y_ref, z_ref, sx_ref):
    nslab = z_ref.shape[0]
    nc = nslab // 2
    kw = Q * SUBW
    nst = w_ref.shape[3] // 2
    spu = nst // LANES
    lhs = [h_ref[0, :, u * kw:(u + 1) * kw] for u in range(NSUB)]
    for u in range(NSUB):
        z = jnp.dot(lhs[u], w_ref[0, u], preferred_element_type=F32)
        for j in range(NSEG):
            for c in range(2 * spu):
                slab = (nc if c >= spu else 0) + u * spu + c % spu
                z_ref[slab, j * PITCH:j * PITCH + SEG, :] = z[j * SEG:(j + 1) * SEG, c * LANES:(c + 1) * LANES]

    half = nc * LANES

    def table(row0, nrows, c, imag):
        lo = (half if imag else 0) + c * LANES
        return pw_ref[0, row0:row0 + nrows, lo:lo + LANES]

    a_re = [jnp.broadcast_to(table(1, 1, c, False), (NSEG, LANES)) for c in range(nc)]
    a_im = [jnp.broadcast_to(table(1, 1, c, True), (NSEG, LANES)) for c in range(nc)]

    def scan_step(t, state):
        new_re, new_im = [], []
        for c in range(nc):
            s_re, s_im = state[c], state[c + nc]
            sx_ref[c, pl.ds(t, NSEG, stride=PITCH), :] = s_re
            sx_ref[c + nc, pl.ds(t, NSEG, stride=PITCH), :] = s_im
            z_re = z_ref[c, pl.ds(t, NSEG, stride=PITCH), :]
            z_im = z_ref[c + nc, pl.ds(t, NSEG, stride=PITCH), :]
            new_re.append(a_re[c] * s_re - a_im[c] * s_im + z_re)
            new_im.append(a_re[c] * s_im + a_im[c] * s_re + z_im)
        return tuple(new_re + new_im)

    zero = jnp.zeros((NSEG, LANES), F32)
    end = lax.fori_loop(0, SEG, scan_step, (zero,) * nslab)

    for c in range(nc):
        ap_re = table(SEG, 1, c, False)
        ap_im = table(SEG, 1, c, True)
        c_re = jnp.zeros((1, LANES), F32)
        c_im = jnp.zeros((1, LANES), F32)
        for j in range(1, NSEG):
            e_re = end[c][j - 1:j]
            e_im = end[c + nc][j - 1:j]
            c_re, c_im = (e_re + ap_re * c_re - ap_im * c_im, e_im + ap_re * c_im + ap_im * c_re)
            p_re = table(0, SEG, c, False)
            p_im = table(0, SEG, c, True)
            rows = slice(j * PITCH, j * PITCH + SEG)
            sx_ref[c, rows, :] = sx_ref[c, rows, :] + (p_re * c_re - p_im * c_im)
            sx_ref[c + nc, rows, :] = sx_ref[c + nc, rows, :] + (p_re * c_im + p_im * c_re)

    nt = kw // MXU_DIM
    for u in range(NSUB):
        slabs = [u * spu + c for c in range(spu)] + [nc + u * spu + c for c in range(spu)]
        sx = jnp.concatenate(
            [jnp.concatenate([sx_ref[c, j * PITCH:j * PITCH + SEG, :] for c in slabs], axis=1)
             for j in range(NSEG)], axis=0).astype(BF16)
        for b in range(nt):
            acc = jnp.dot(lhs[u][:, :(b + 1) * MXU_DIM], r_ref[0, u, (nt - 1 - b) * MXU_DIM:, :],
                          preferred_element_type=F32)
            acc = acc + lax.dot_general(sx, vt_ref[0, u, b * MXU_DIM:(b + 1) * MXU_DIM, :],
                                        (((1,), (1,)), ((), ())), preferred_element_type=F32)
            y_ref[0, :, u * kw + b * MXU_DIM:u * kw + (b + 1) * MXU_DIM] = acc.astype(BF16)


def _s5_out_kernel(x_ref, y_ref, w_ref, b_ref, o_ref, g_ref):
    mc = y_ref.shape[1]
    c0 = GELU_C0
    c1 = GELU_C0 * 0.044715
    lane = lax.broadcasted_iota(jnp.int32, (mc, LANES), 1)
    for gb in range(NGB):
        for i in range(Q // NSUB):
            sub = [y_ref[gb, :, u * Q * SUBW + i * LANES:u * Q * SUBW + (i + 1) * LANES].astype(F32)
                   for u in range(NSUB)]
            for r in range(NSUB):
                y = None
                for u in range(NSUB):
                    t = sub[u] if u == r else pltpu.roll(sub[u], ((u - r) * SUBW) % LANES, 1)
                    y = t if y is None else jnp.where(
                        jnp.logical_and(lane >= u * SUBW, lane < (u + 1) * SUBW), t, y)
                inner = y * (c0 + c1 * (y * y))
                g_ref[gb, pl.ds(NSUB * i + r, mc, stride=TOK_PITCH), :] = (0.25 * y) * (1.0 + jnp.tanh(inner))
    gh = jnp.concatenate(
        [jnp.concatenate([g_ref[gb, k * TOK_PITCH:k * TOK_PITCH + Q, :] for k in range(mc)], axis=0)
         for gb in range(NGB)], axis=1)
    half_gate = jnp.dot(gh.astype(BF16), w_ref[...], preferred_element_type=F32) + b_ref[...]
    o_ref[...] = x_ref[...] + gh * (1.0 + jnp.tanh(half_gate))


def _s5_prep_kernel(lr_ref, li_ref, ldt_ref, be_re_ref, be_im_ref, ce_re_ref, ce_im_ref, d_ref,
                    w_ref, vt_ref, r_ref, pw_ref, wf_ref):
    nst = lr_ref.shape[2]
    nsu = nst // NSUB
    lr, li = lr_ref[0], li_ref[0]
    dt = jnp.exp(ldt_ref[0])
    th_re, th_im = lr * dt, li * dt

    def apow(n):
        mag = jnp.exp(n * th_re)
        return mag * jnp.cos(n * th_im), mag * jnp.sin(n * th_im)

    n_small = lax.broadcasted_iota(jnp.int32, (3 * SUBLANES, 1), 0).astype(F32)
    p_re, p_im = apow(n_small)
    a_re, a_im = p_re[1:2], p_im[1:2]
    den = lr * lr + li * li
    z_re = ((a_re - 1.0) * lr + a_im * li) / den
    z_im = (a_im * lr - (a_re - 1.0) * li) / den
    bb_re, bb_im = _cmul(z_re, z_im, be_re_ref[0], be_im_ref[0])
    ce_re, ce_im = ce_re_ref[0], ce_im_ref[0]

    for s in range(Q):
        rows = slice(s * LANES, (s + 1) * LANES)
        n = Q - 1 - s
        w_re, w_im = _cmul(bb_re, bb_im, p_re[n:n + 1], p_im[n:n + 1])
        wf_ref[rows, :nst] = w_re
        wf_ref[rows, nst:] = w_im
        v_re, v_im = _cmul(ce_re, ce_im, p_re[s + 1:s + 2], p_im[s + 1:s + 2])
        for u in range(NSUB):
            src_rows = slice(u * SUBW, (u + 1) * SUBW)
            src_cols = slice(u * nsu, (u + 1) * nsu)
            dst_rows = slice(s * SUBW, (s + 1) * SUBW)
            w_ref[0, u, dst_rows, :nsu] = w_re[src_rows, src_cols].astype(BF16)
            w_ref[0, u, dst_rows, nsu:] = w_im[src_rows, src_cols].astype(BF16)
            vt_ref[0, u, dst_rows, :nsu] = v_re[src_rows, src_cols].astype(BF16)
            vt_ref[0, u, dst_rows, nsu:] = (-v_im[src_rows, src_cols]).astype(BF16)

    def split(a):
        hi = a.astype(BF16)
        return hi, (a - hi.astype(F32)).astype(BF16)

    def dot_t(a, b):
        return lax.dot_general(a, b, (((1,), (1,)), ((), ())), preferred_element_type=F32)

    w_hi, w_lo = split(wf_ref[...])
    c_hi, c_lo = split(jnp.concatenate([ce_re, -ce_im], axis=1))
    kall = dot_t(w_hi, c_hi) + dot_t(w_lo, c_hi) + dot_t(w_hi, c_lo)
    ri = lax.broadcasted_iota(jnp.int32, (LANES, LANES), 0)
    ci = lax.broadcasted_iota(jnp.int32, (LANES, LANES), 1)
    lag0 = kall[(Q - 1) * LANES:] + jnp.where(ri == ci, d_ref[0], 0.0)
    kall = jnp.concatenate([kall[:(Q - 1) * LANES], lag0], axis=0)

    rolled = [kall] + [pltpu.roll(kall, dd * SUBW, 1) for dd in range(1, NSUB)]
    lane = lax.broadcasted_iota(jnp.int32, (Q * SUBW, LANES), 1)
    for u in range(NSUB):
        sub_rows = [jnp.concatenate([x[s * LANES + u * SUBW:s * LANES + (u + 1) * SUBW] for s in range(Q)], axis=0)
                    for x in rolled]
        for piece in range(MXU_DIM // LANES):
            acc = None
            for r in range(NSUB):
                up = (TPT - 1 - (piece * NSUB + r)) * SUBW
                x = sub_rows[(r - u) % NSUB]
                if up:
                    x = jnp.concatenate([x[up:], jnp.zeros((up, LANES), F32)], axis=0)
                acc = x if acc is None else jnp.where(
                    jnp.logical_and(lane >= r * SUBW, lane < (r + 1) * SUBW), x, acc)
            r_ref[0, u, :, piece * LANES:(piece + 1) * LANES] = acc.astype(BF16)

    n_big = float(Q) * lax.broadcasted_iota(jnp.int32, (pw_ref.shape[1], 1), 0).astype(F32)
    t_re, t_im = apow(n_big)
    pw_ref[0, :, :nst] = t_re
    pw_ref[0, :, nst:] = t_im


def _s5_tables(lam_re, lam_im, log_dt, b_re, b_im, c_re, c_im, d_skip):
    G, P = lam_re.shape
    C = b_re.shape[-1]
    nst = GPB * P
    eye = jnp.eye(GPB, dtype=bool)

    def expand(t):
        t = t.astype(F32).reshape(NGB, GPB, C, P)
        t = jnp.where(eye[None, :, None, :, None], t[:, :, :, None, :], 0.0)
        return t.reshape(NGB, GPB * C, nst)

    def lanes(t):
        return t.astype(F32).reshape(NGB, 1, nst)

    args = (lanes(lam_re), lanes(lam_im), lanes(jnp.broadcast_to(log_dt[:, None], (G, P))),
            expand(jnp.swapaxes(b_re, 1, 2)), expand(jnp.swapaxes(b_im, 1, 2)), expand(c_re), expand(c_im),
            d_skip.astype(F32).reshape(NGB, 1, LANES))
    row_spec = pl.BlockSpec((1, 1, nst), lambda gb: (gb, 0, 0))
    mat_spec = pl.BlockSpec((1, LANES, nst), lambda gb: (gb, 0, 0))
    n_pw = SEG + SUBLANES
    nsu = nst // NSUB
    return pl.pallas_call(
        _s5_prep_kernel,
        grid=(NGB,),
        in_specs=[row_spec, row_spec, row_spec, mat_spec, mat_spec, mat_spec, mat_spec,
                  pl.BlockSpec((1, 1, LANES), lambda gb: (gb, 0, 0))],
        out_specs=[
            pl.BlockSpec((1, NSUB, Q * SUBW, 2 * nsu), lambda gb: (gb, 0, 0, 0)),
            pl.BlockSpec((1, NSUB, Q * SUBW, 2 * nsu), lambda gb: (gb, 0, 0, 0)),
            pl.BlockSpec((1, NSUB, Q * SUBW, MXU_DIM), lambda gb: (gb, 0, 0, 0)),
            pl.BlockSpec((1, n_pw, 2 * nst), lambda gb: (gb, 0, 0)),
        ],
        out_shape=[
            jax.ShapeDtypeStruct((NGB, NSUB, Q * SUBW, 2 * nsu), BF16),
            jax.ShapeDtypeStruct((NGB, NSUB, Q * SUBW, 2 * nsu), BF16),
            jax.ShapeDtypeStruct((NGB, NSUB, Q * SUBW, MXU_DIM), BF16),
            jax.ShapeDtypeStruct((NGB, n_pw, 2 * nst), F32),
        ],
        scratch_shapes=[pltpu.VMEM((Q * LANES, 2 * nst), F32)],
        compiler_params=pltpu.CompilerParams(dimension_semantics=("arbitrary",), vmem_limit_bytes=VMEM_LIMIT),
        name="s5_prep",
    )(*args)


def _s5_layer(x, g, lam_re, lam_im, log_dt, b_re, b_im, c_re, c_im, d_skip, w_glu, b_glu, *, ma=64, mc=64):
    bsz, L, d = x.shape
    rows = bsz * L // Q
    rows_b = L // Q
    assert rows_b == NSEG * SEG and d == NGB * LANES
    w, vt, r_rev, pw = _s5_tables(lam_re, lam_im, log_dt, b_re, b_im, c_re, c_im, d_skip)
    xt = x.reshape(bsz * L, d)
    cp = pltpu.CompilerParams(dimension_semantics=("arbitrary",), vmem_limit_bytes=VMEM_LIMIT)

    h = pl.pallas_call(
        _s5_norm_kernel,
        grid=(rows // ma,),
        in_specs=[pl.BlockSpec((ma * Q, d), lambda i: (i, 0)), _const_spec((1, d))],
        out_specs=pl.BlockSpec((NGB, ma, Q * LANES), lambda i: (0, i, 0)),
        out_shape=jax.ShapeDtypeStruct((NGB, rows, Q * LANES), BF16),
        scratch_shapes=[pltpu.VMEM((NGB, ma * TOK_PITCH, LANES), F32)],
        compiler_params=cp,
        name="s5_norm",
    )(xt, g.reshape(1, d))

    nstate = pw.shape[-1]
    nslab = nstate // LANES
    y = pl.pallas_call(
        _s5_core_kernel,
        grid=(NGB, bsz),
        in_specs=[
            pl.BlockSpec((1, rows_b, Q * LANES), lambda gb, b: (gb, b, 0)),
            pl.BlockSpec((1,) + w.shape[1:], lambda gb, b: (gb, 0, 0, 0)),
            pl.BlockSpec((1,) + r_rev.shape[1:], lambda gb, b: (gb, 0, 0, 0)),
            pl.BlockSpec((1,) + vt.shape[1:], lambda gb, b: (gb, 0, 0, 0)),
            pl.BlockSpec((1, pw.shape[1], nstate), lambda gb, b: (gb, 0, 0)),
        ],
        out_specs=pl.BlockSpec((1, rows_b, Q * LANES), lambda gb, b: (gb, b, 0)),
        out_shape=jax.ShapeDtypeStruct((NGB, rows, Q * LANES), BF16),
        scratch_shapes=[
            pltpu.VMEM((nslab, NSEG * PITCH, LANES), F32),
            pltpu.VMEM((nslab, NSEG * PITCH, LANES), F32),
        ],
        compiler_params=pltpu.CompilerParams(
            dimension_semantics=("arbitrary", "arbitrary"), vmem_limit_bytes=VMEM_LIMIT),
        name="s5_core",
    )(h, w, r_rev, vt, pw)

    out = pl.pallas_call(
        _s5_out_kernel,
        grid=(rows // mc,),
        in_specs=[
            pl.BlockSpec((mc * Q, d), lambda i: (i, 0)),
            pl.BlockSpec((NGB, mc, Q * LANES), lambda i: (0, i, 0)),
            _const_spec((d, d)),
            _const_spec((1, d)),
        ],
        out_specs=pl.BlockSpec((mc * Q, d), lambda i: (i, 0)),
        out_shape=jax.ShapeDtypeStruct((bsz * L, d), F32),
        scratch_shapes=[pltpu.VMEM((NGB, mc * TOK_PITCH, LANES), F32)],
        compiler_params=cp,
        name="s5_out",
    )(xt, y, w_glu.astype(BF16), (0.5 * b_glu).reshape(1, d))
    return out.reshape(bsz, L, d)


def kernel(x, positions, norm_mix, norm_ffn, norm_final, s5_lambda_re, s5_lambda_im, s5_log_dt, s5_b_re, s5_b_im, s5_c_re, s5_c_im, s5_d, s5_w_glu, s5_b_glu, attn_w_qkv, attn_b_qkv, attn_sinks, attn_w_o, attn_b_o, ffn_w_up, ffn_w_conv, ffn_b_conv, ffn_w_down):
    depth = norm_mix.shape[0]
    w_up = ffn_w_up.astype(BF16)
    w_down = ffn_w_down.astype(BF16)
    for i in range(depth):
        j = i // 2
        if i % 2 == 0:
            x = _s5_layer(x, norm_mix[i], s5_lambda_re[j], s5_lambda_im[j], s5_log_dt[j],
                          s5_b_re[j], s5_b_im[j], s5_c_re[j], s5_c_im[j], s5_d[j], s5_w_glu[j], s5_b_glu[j])
        else:
            x = _swa_layer(x, positions, norm_mix[i], attn_w_qkv[j], attn_b_qkv[j], attn_sinks[j],
                           attn_w_o[j], attn_b_o[j])
        x = _conv_ffn(x, norm_ffn[i], w_up, ffn_w_conv[i], ffn_b_conv[i], w_down, i,
                      norm_final if i == depth - 1 else None)
    return x
```

```python
import functools

import numpy as np
import jax
import jax.numpy as jnp
from jax import lax
from jax.experimental import pallas as pl
from jax.experimental.pallas import tpu as pltpu

F32 = jnp.float32
BF16 = jnp.bfloat16

EPS = 1e-5
NEG_INF = -1e30
LOG2E = 1.4426950408889634
GELU_C0 = float(np.sqrt(2.0 / np.pi).astype(np.float32))

HEAD_DIM = 64
N_KV_HEADS = 4
ROPE_DIM = 16
ROPE_THETA = 500000.0
ATTN_BLOCK = 128
S5_GROUP = 16

LANES = 128
SUBLANES = 8
MXU_DIM = 256
VMEM_LIMIT = 56 * 1024 * 1024

Q = 16
NGB = 8
GPB = LANES // S5_GROUP
NSUB = 4
SUBW = LANES // NSUB
TPT = MXU_DIM // SUBW
SEG = 32
NSEG = 16
PITCH = 40
TOK_PITCH = 24


def _rms(x, g):
    ms = jnp.mean(x * x, axis=-1, keepdims=True)
    return x * lax.rsqrt(ms + EPS) * g


def _cmul(ar, ai, br, bi):
    return ar * br - ai * bi, ar * bi + ai * br


def _const_spec(shape):
    nd = len(shape)
    return pl.BlockSpec(shape, lambda *_: (0,) * nd, pipeline_mode=pl.Buffered(1))


def _ffn_kernel(x_ref, g_ref, wu_ref, wc_ref, bc_ref, wd_ref, gf_ref, o_ref, act_ref, carry_ref,
                *, fb, final_norm):
    tm = x_ref.shape[1]
    d_ff = wd_ref.shape[0]

    @pl.when(pl.program_id(1) == 0)
    def _():
        carry_ref[...] = jnp.zeros_like(carry_ref)

    x = x_ref[0]
    half = tm // 2
    h_top = _rms(x_ref[0, :half], g_ref[...]).astype(BF16)
    h_bot = _rms(x_ref[0, half:], g_ref[...]).astype(BF16)
    rows = lax.broadcasted_iota(jnp.int32, (SUBLANES, fb), 0)

    def conv_block(col):
        u = jnp.concatenate([jnp.dot(h_top, wu_ref[:, col:col + fb], preferred_element_type=F32),
                             jnp.dot(h_bot, wu_ref[:, col:col + fb], preferred_element_type=F32)], axis=0)
        prev = carry_ref[:, col:col + fb]
        carry_ref[:, col:col + fb] = u[tm - SUBLANES:, :]
        s1 = pltpu.roll(u, 1, 0)
        s2 = pltpu.roll(u, 2, 0)
        t1 = jnp.where(rows < 1, pltpu.roll(prev, 1, 0), s1[:SUBLANES])
        t2 = jnp.where(rows < 2, pltpu.roll(prev, 2, 0), s2[:SUBLANES])
        s1 = jnp.concatenate([t1, s1[SUBLANES:]], axis=0)
        s2 = jnp.concatenate([t2, s2[SUBLANES:]], axis=0)
        w = wc_ref[:, col:col + fb]
        return w[0:1] * s2 + w[1:2] * s1 + w[2:3] * u + bc_ref[:, col:col + fb]

    for j in range(d_ff // fb):
        a = conv_block(j * fb)
        v = conv_block(d_ff + j * fb)
        act_ref[:, j * fb:(j + 1) * fb] = (a * (1.0 + jnp.tanh(a)) * v).astype(BF16)

    y = jnp.dot(act_ref[...], wd_ref[...], preferred_element_type=F32) + x
    if final_norm:
        y = _rms(y, gf_ref[...])
    o_ref[0] = y


def _layer_spec(shape, layer):
    nd = len(shape)
    return pl.BlockSpec((None,) + shape, lambda *_: (layer,) + (0,) * nd, pipeline_mode=pl.Buffered(1))


def _conv_ffn(x, g, w_up, w_conv, b_conv, w_down, layer, g_final, *, tm=1024, fb=256):
    bsz, L, d = x.shape
    d_ff = w_down.shape[1]
    final_norm = g_final is not None
    gf = g_final if final_norm else g
    kern = functools.partial(_ffn_kernel, fb=fb, final_norm=final_norm)
    half_gate = jnp.where(jnp.arange(2 * d_ff) < d_ff, 0.5, 1.0).astype(F32)
    w_conv = w_conv * half_gate[None, :]
    b_conv = b_conv * half_gate
    return pl.pallas_call(
        kern,
        grid=(bsz, L // tm),
        in_specs=[
            pl.BlockSpec((1, tm, d), lambda b, i: (b, i, 0)),
            _const_spec((1, d)),
            _layer_spec((d, 2 * d_ff), layer),
            _const_spec((3, 2 * d_ff)),
            _const_spec((1, 2 * d_ff)),
            _layer_spec((d_ff, d), layer),
            _const_spec((1, d)),
        ],
        out_specs=pl.BlockSpec((1, tm, d), lambda b, i: (b, i, 0)),
        out_shape=jax.ShapeDtypeStruct(x.shape, F32),
        scratch_shapes=[
            pltpu.VMEM((tm, d_ff), BF16),
            pltpu.VMEM((SUBLANES, 2 * d_ff), F32),
        ],
        compiler_params=pltpu.CompilerParams(
            dimension_semantics=("arbitrary", "arbitrary"), vmem_limit_bytes=VMEM_LIMIT),
        name="conv_ffn",
    )(x, g.reshape(1, d), w_up, w_conv, b_conv.reshape(1, -1), w_down, gf.reshape(1, d))


def _qkv_kernel(x_ref, pos_ref, g_ref, wt_ref, bt_ref, fr_ref, qt_ref, k_ref, vt_ref):
    d = x_ref.shape[2]
    nkv = N_KV_HEADS * HEAD_DIM
    half = ROPE_DIM // 2
    h = _rms(x_ref[0], g_ref[...]).astype(BF16)
    qkvt = lax.dot_general(wt_ref[...], h, (((1,), (1,)), ((), ())), preferred_element_type=F32) + bt_ref[...]
    ang = fr_ref[...] * pos_ref[0].astype(F32)
    cos = jnp.cos(ang)
    sin = jnp.sin(ang)

    def rotate(base):
        t1 = qkvt[base:base + half]
        t2 = qkvt[base + half:base + ROPE_DIM]
        return jnp.concatenate([t1 * cos - t2 * sin, t2 * cos + t1 * sin, qkvt[base + ROPE_DIM:base + HEAD_DIM]],
                               axis=0)

    for hq in range(d // HEAD_DIM):
        qt_ref[0, hq * HEAD_DIM:(hq + 1) * HEAD_DIM, :] = rotate(hq * HEAD_DIM).astype(BF16)
    kt = jnp.concatenate([rotate(d + hk * HEAD_DIM) for hk in range(N_KV_HEADS)], axis=0)
    k_ref[0] = kt.T.astype(BF16)
    vt_ref[0] = qkvt[d + nkv:].astype(BF16)


def _attn_kernel(sink_ref, qt_ref, kc_ref, kp_ref, vtc_ref, vtp_ref, x_ref, wo_ref, bo_ref, o_ref, ot_ref):
    tq = x_ref.shape[1]
    blk = ATTN_BLOCK
    q_per_kv = qt_ref.shape[1] // (N_KV_HEADS * HEAD_DIM)
    first_tile = pl.program_id(1) == 0
    key = lax.broadcasted_iota(jnp.int32, (2 * blk, q_per_kv * blk), 0)
    col = lax.broadcasted_iota(jnp.int32, (2 * blk, q_per_kv * blk), 1)
    qi = col & (blk - 1)
    valid = jnp.logical_or(jnp.logical_and(key < blk, key > qi), jnp.logical_and(key >= blk, key - blk <= qi))
    valid_first = jnp.logical_and(valid, jnp.logical_or(key >= blk, jnp.logical_not(first_tile)))
    bias = jnp.where(valid, 0.0, NEG_INF).astype(F32)
    bias_first = jnp.where(valid_first, 0.0, NEG_INF).astype(F32)
    col1 = lax.broadcasted_iota(jnp.int32, (1, q_per_kv * blk), 1)
    sinks = []
    for hk in range(N_KV_HEADS):
        sk = jnp.zeros(col1.shape, F32)
        for g in range(q_per_kv):
            sk = jnp.where(jnp.logical_and(col1 >= g * blk, col1 < (g + 1) * blk),
                           sink_ref[hk * q_per_kv + g] * LOG2E, sk)
        sinks.append(sk)

    def scores(n, hk):
        cols = slice(n * blk, (n + 1) * blk)
        if n == 0:
            k2 = jnp.concatenate([kp_ref[0], kc_ref[0, cols, :]], axis=0)
            mask = bias_first
        else:
            k2 = kc_ref[0, (n - 1) * blk:(n + 1) * blk, :]
            mask = bias
        heads = [hk * q_per_kv + g for g in range(q_per_kv)]
        kk = k2[:, hk * HEAD_DIM:(hk + 1) * HEAD_DIM]
        qc = jnp.concatenate([qt_ref[0, hq * HEAD_DIM:(hq + 1) * HEAD_DIM, cols] for hq in heads], axis=1)
        return jnp.dot(kk, qc, preferred_element_type=F32) + mask

    def finish(n, hk, s):
        cols = slice(n * blk, (n + 1) * blk)
        if n == 0:
            v2t = jnp.concatenate([vtp_ref[0], vtc_ref[0, :, cols]], axis=1)
        else:
            v2t = vtc_ref[0, :, (n - 1) * blk:(n + 1) * blk]
        heads = [hk * q_per_kv + g for g in range(q_per_kv)]
        vv = v2t[hk * HEAD_DIM:(hk + 1) * HEAD_DIM, :]
        m = jnp.maximum(jnp.max(s, axis=0, keepdims=True), sinks[hk])
        p = jnp.exp2(s - m)
        inv = 1.0 / (jnp.sum(p, axis=0, keepdims=True) + jnp.exp2(sinks[hk] - m))
        ot = jnp.dot(vv, p.astype(BF16), preferred_element_type=F32) * inv
        for g, hq in enumerate(heads):
            ot_ref[hq * HEAD_DIM:(hq + 1) * HEAD_DIM, cols] = ot[:, g * blk:(g + 1) * blk].astype(BF16)

    units = [(n, hk) for n in range(tq // blk) for hk in range(N_KV_HEADS)]
    s_cur = scores(*units[0])
    for i, unit in enumerate(units):
        s_next = scores(*units[i + 1]) if i + 1 < len(units) else None
        finish(*unit, s_cur)
        s_cur = s_next
    attn = lax.dot_general(ot_ref[...], wo_ref[...], (((0,), (0,)), ((), ())), preferred_element_type=F32)
    o_ref[0] = attn + bo_ref[...] + x_ref[0]


def _swa_layer(x, pos, g, w_qkv, b_qkv, sinks, w_o, b_o, *, tm=1024, tq=512):
    bsz, L, d = x.shape
    nkv = N_KV_HEADS * HEAD_DIM
    qkv_dim = w_qkv.shape[1]
    inv_freq = 1.0 / jnp.power(ROPE_THETA, jnp.arange(0, ROPE_DIM, 2, dtype=F32) / ROPE_DIM)
    scale = jnp.where(jnp.arange(qkv_dim) < d, LOG2E * HEAD_DIM ** -0.5, 1.0).astype(F32)
    wt = (w_qkv * scale[None, :]).T.astype(BF16)
    bt = (b_qkv * scale).astype(F32).reshape(qkv_dim, 1)

    qt, k, vt = pl.pallas_call(
        _qkv_kernel,
        grid=(bsz, L // tm),
        in_specs=[
            pl.BlockSpec((1, tm, d), lambda b, i: (b, i, 0)),
            pl.BlockSpec((1, 1, tm), lambda b, i: (b, 0, i)),
            _const_spec((1, d)),
            _const_spec((qkv_dim, d)),
            _const_spec((qkv_dim, 1)),
            _const_spec((ROPE_DIM // 2, 1)),
        ],
        out_specs=[
            pl.BlockSpec((1, d, tm), lambda b, i: (b, 0, i)),
            pl.BlockSpec((1, tm, nkv), lambda b, i: (b, i, 0)),
            pl.BlockSpec((1, nkv, tm), lambda b, i: (b, 0, i)),
        ],
        out_shape=[
            jax.ShapeDtypeStruct((bsz, d, L), BF16),
            jax.ShapeDtypeStruct((bsz, L, nkv), BF16),
            jax.ShapeDtypeStruct((bsz, nkv, L), BF16),
        ],
        compiler_params=pltpu.CompilerParams(
            dimension_semantics=("arbitrary", "arbitrary"), vmem_limit_bytes=VMEM_LIMIT),
        name="qkv_rope",
    )(x, pos.reshape(bsz, 1, L), g.reshape(1, d), wt, bt, inv_freq.reshape(-1, 1))

    bpt = tq // ATTN_BLOCK
    prev_block = lambda i: jnp.maximum(i * bpt - 1, 0)
    return pl.pallas_call(
        _attn_kernel,
        grid=(bsz, L // tq),
        in_specs=[
            pl.BlockSpec(memory_space=pltpu.SMEM),
            pl.BlockSpec((1, d, tq), lambda b, i: (b, 0, i)),
            pl.BlockSpec((1, tq, nkv), lambda b, i: (b, i, 0)),
            pl.BlockSpec((1, ATTN_BLOCK, nkv), lambda b, i: (b, prev_block(i), 0)),
            pl.BlockSpec((1, nkv, tq), lambda b, i: (b, 0, i)),
            pl.BlockSpec((1, nkv, ATTN_BLOCK), lambda b, i: (b, 0, prev_block(i))),
            pl.BlockSpec((1, tq, d), lambda b, i: (b, i, 0)),
            _const_spec((d, d)),
            _const_spec((1, d)),
        ],
        out_specs=pl.BlockSpec((1, tq, d), lambda b, i: (b, i, 0)),
        out_shape=jax.ShapeDtypeStruct(x.shape, F32),
        scratch_shapes=[pltpu.VMEM((d, tq), BF16)],
        compiler_params=pltpu.CompilerParams(
            dimension_semantics=("arbitrary", "arbitrary"), vmem_limit_bytes=VMEM_LIMIT),
        name="swa_attn",
    )(sinks.astype(F32), qt, k, k, vt, vt, x, w_o.astype(BF16), b_o.reshape(1, d))


def _s5_norm_kernel(x_ref, g_ref, h_ref, hs_ref):
    ma = h_ref.shape[1]
    hs = _rms(x_ref[...], g_ref[...])
    for gb in range(NGB):
        for k in range(ma):
            hs_ref[gb, k * TOK_PITCH:k * TOK_PITCH + Q, :] = hs[k * Q:(k + 1) * Q, gb * LANES:(gb + 1) * LANES]
    lane = lax.broadcasted_iota(jnp.int32, (ma, LANES), 1)
    for gb in range(NGB):
        for i in range(Q // NSUB):
            tok = [hs_ref[gb, pl.ds(NSUB * i + r, ma, stride=TOK_PITCH), :] for r in range(NSUB)]
            for u in range(NSUB):
                acc = None
                for r in range(NSUB):
                    t = tok[r] if r == u else pltpu.roll(tok[r], ((r - u) * SUBW) % LANES, 1)
                    acc = t if acc is None else jnp.where(
                        jnp.logical_and(lane >= r * SUBW, lane < (r + 1) * SUBW), t, acc)
                col = u * Q * SUBW + i * LANES
                h_ref[gb, :, col:col + LANES] = acc.astype(BF16)


def _s5_core_kernel(h_ref, w_ref, r_ref, vt_ref, pw_ref, y_ref, z_ref, sx_ref):
    nslab = z_ref.shape[0]
    nc = nslab // 2
    kw = Q * SUBW
    nst = w_ref.shape[3] // 2
    spu = nst // LANES
    lhs = [h_ref[0, :, u * kw:(u + 1) * kw] for u in range(NSUB)]
    for u in range(NSUB):
        z = jnp.dot(lhs[u], w_ref[0, u], preferred_element_type=F32)
        for j in range(NSEG):
            for c in range(2 * spu):
                slab = (nc if c >= spu else 0) + u * spu + c % spu
                z_ref[slab, j * PITCH:j * PITCH + SEG, :] = z[j * SEG:(j + 1) * SEG, c * LANES:(c + 1) * LANES]

    half = nc * LANES

    def table(row0, nrows, c, imag):
        lo = (half if imag else 0) + c * LANES
        return pw_ref[0, row0:row0 + nrows, lo:lo + LANES]

    a_re = [jnp.broadcast_to(table(1, 1, c, False), (NSEG, LANES)) for c in range(nc)]
    a_im = [jnp.broadcast_to(table(1, 1, c, True), (NSEG, LANES)) for c in range(nc)]

    def scan_step(t, state):
        new_re, new_im = [], []
        for c in range(nc):
            s_re, s_im = state[c], state[c + nc]
            sx_ref[c, pl.ds(t, NSEG, stride=PITCH), :] = s_re
            sx_ref[c + nc, pl.ds(t, NSEG, stride=PITCH), :] = s_im
            z_re = z_ref[c, pl.ds(t, NSEG, stride=PITCH), :]
            z_im = z_ref[c + nc, pl.ds(t, NSEG, stride=PITCH), :]
            new_re.append(a_re[c] * s_re - a_im[c] * s_im + z_re)
            new_im.append(a_re[c] * s_im + a_im[c] * s_re + z_im)
        return tuple(new_re + new_im)

    zero = jnp.zeros((NSEG, LANES), F32)
    end = lax.fori_loop(0, SEG, scan_step, (zero,) * nslab)

    for c in range(nc):
        ap_re = table(SEG, 1, c, False)
        ap_im = table(SEG, 1, c, True)
        c_re = jnp.zeros((1, LANES), F32)
        c_im = jnp.zeros((1, LANES), F32)
        for j in range(1, NSEG):
            e_re = end[c][j - 1:j]
            e_im = end[c + nc][j - 1:j]
            c_re, c_im = (e_re + ap_re * c_re - ap_im * c_im, e_im + ap_re * c_im + ap_im * c_re)
            p_re = table(0, SEG, c, False)
            p_im = table(0, SEG, c, True)
            rows = slice(j * PITCH, j * PITCH + SEG)
            sx_ref[c, rows, :] = sx_ref[c, rows, :] + (p_re * c_re - p_im * c_im)
            sx_ref[c + nc, rows, :] = sx_ref[c + nc, rows, :] + (p_re * c_im + p_im * c_re)

    nt = kw // MXU_DIM
    for u in range(NSUB):
        slabs = [u * spu + c for c in range(spu)] + [nc + u * spu + c for c in range(spu)]
        sx = jnp.concatenate(
            [jnp.concatenate([sx_ref[c, j * PITCH:j * PITCH + SEG, :] for c in slabs], axis=1)
             for j in range(NSEG)], axis=0).astype(BF16)
        for b in range(nt):
            acc = jnp.dot(lhs[u][:, :(b + 1) * MXU_DIM], r_ref[0, u, (nt - 1 - b) * MXU_DIM:, :],
                          preferred_element_type=F32)
            acc = acc + lax.dot_general(sx, vt_ref[0, u, b * MXU_DIM:(b + 1) * MXU_DIM, :],
                                        (((1,), (1,)), ((), ())), preferred_element_type=F32)
            y_ref[0, :, u * kw + b * MXU_DIM:u * kw + (b + 1) * MXU_DIM] = acc.astype(BF16)


def _s5_out_kernel(x_ref, y_ref, w_ref, b_ref, o_ref, g_ref):
    mc = y_ref.shape[1]
    c0 = GELU_C0
    c1 = GELU_C0 * 0.044715
    lane = lax.broadcasted_iota(jnp.int32, (mc, LANES), 1)
    for gb in range(NGB):
        for i in range(Q // NSUB):
            sub = [y_ref[gb, :, u * Q * SUBW + i * LANES:u * Q * SUBW + (i + 1) * LANES].astype(F32)
                   for u in range(NSUB)]
            for r in range(NSUB):
                y = None
                for u in range(NSUB):
                    t = sub[u] if u == r else pltpu.roll(sub[u], ((u - r) * SUBW) % LANES, 1)
                    y = t if y is None else jnp.where(
                        jnp.logical_and(lane >= u * SUBW, lane < (u + 1) * SUBW), t, y)
                inner = y * (c0 + c1 * (y * y))
                g_ref[gb, pl.ds(NSUB * i + r, mc, stride=TOK_PITCH), :] = (0.25 * y) * (1.0 + jnp.tanh(inner))
    gh = jnp.concatenate(
        [jnp.concatenate([g_ref[gb, k * TOK_PITCH:k * TOK_PITCH + Q, :] for k in range(mc)], axis=0)
         for gb in range(NGB)], axis=1)
    half_gate = jnp.dot(gh.astype(BF16), w_ref[...], preferred_element_type=F32) + b_ref[...]
    o_ref[...] = x_ref[...] + gh * (1.0 + jnp.tanh(half_gate))


def _s5_prep_kernel(lr_ref, li_ref, ldt_ref, be_re_ref, be_im_ref, ce_re_ref, ce_im_ref, d_ref,
                    w_ref, vt_ref, r_ref, pw_ref, wf_ref):
    nst = lr_ref.shape[2]
    nsu = nst // NSUB
    lr, li = lr_ref[0], li_ref[0]
    dt = jnp.exp(ldt_ref[0])
    th_re, th_im = lr * dt, li * dt

    def apow(n):
        mag = jnp.exp(n * th_re)
        return mag * jnp.cos(n * th_im), mag * jnp.sin(n * th_im)

    n_small = lax.broadcasted_iota(jnp.int32, (3 * SUBLANES, 1), 0).astype(F32)
    p_re, p_im = apow(n_small)
    a_re, a_im = p_re[1:2], p_im[1:2]
    den = lr * lr + li * li
    z_re = ((a_re - 1.0) * lr + a_im * li) / den
    z_im = (a_im * lr - (a_re - 1.0) * li) / den
    bb_re, bb_im = _cmul(z_re, z_im, be_re_ref[0], be_im_ref[0])
    ce_re, ce_im = ce_re_ref[0], ce_im_ref[0]

    for s in range(Q):
        rows = slice(s * LANES, (s + 1) * LANES)
        n = Q - 1 - s
        w_re, w_im = _cmul(bb_re, bb_im, p_re[n:n + 1], p_im[n:n + 1])
        wf_ref[rows, :nst] = w_re
        wf_ref[rows, nst:] = w_im
        v_re, v_im = _cmul(ce_re, ce_im, p_re[s + 1:s + 2], p_im[s + 1:s + 2])
        for u in range(NSUB):
            src_rows = slice(u * SUBW, (u + 1) * SUBW)
            src_cols = slice(u * nsu, (u + 1) * nsu)
            dst_rows = slice(s * SUBW, (s + 1) * SUBW)
            w_ref[0, u, dst_rows, :nsu] = w_re[src_rows, src_cols].astype(BF16)
            w_ref[0, u, dst_rows, nsu:] = w_im[src_rows, src_cols].astype(BF16)
            vt_ref[0, u, dst_rows, :nsu] = v_re[src_rows, src_cols].astype(BF16)
            vt_ref[0, u, dst_rows, nsu:] = (-v_im[src_rows, src_cols]).astype(BF16)

    def split(a):
        hi = a.astype(BF16)
        return hi, (a - hi.astype(F32)).astype(BF16)

    def dot_t(a, b):
        return lax.dot_general(a, b, (((1,), (1,)), ((), ())), preferred_element_type=F32)

    w_hi, w_lo = split(wf_ref[...])
    c_hi, c_lo = split(jnp.concatenate([ce_re, -ce_im], axis=1))
    kall = dot_t(w_hi, c_hi) + dot_t(w_lo, c_hi) + dot_t(w_hi, c_lo)
    ri = lax.broadcasted_iota(jnp.int32, (LANES, LANES), 0)
    ci = lax.broadcasted_iota(jnp.int32, (LANES, LANES), 1)
    lag0 = kall[(Q - 1) * LANES:] + jnp.where(ri == ci, d_ref[0], 0.0)
    kall = jnp.concatenate([kall[:(Q - 1) * LANES], lag0], axis=0)

    rolled = [kall] + [pltpu.roll(kall, dd * SUBW, 1) for dd in range(1, NSUB)]
    lane = lax.broadcasted_iota(jnp.int32, (Q * SUBW, LANES), 1)
    for u in range(NSUB):
        sub_rows = [jnp.concatenate([x[s * LANES + u * SUBW:s * LANES + (u + 1) * SUBW] for s in range(Q)], axis=0)
                    for x in rolled]
        for piece in range(MXU_DIM // LANES):
            acc = None
            for r in range(NSUB):
                up = (TPT - 1 - (piece * NSUB + r)) * SUBW
                x = sub_rows[(r - u) % NSUB]
                if up:
                    x = jnp.concatenate([x[up:], jnp.zeros((up, LANES), F32)], axis=0)
                acc = x if acc is None else jnp.where(
                    jnp.logical_and(lane >= r * SUBW, lane < (r + 1) * SUBW), x, acc)
            r_ref[0, u, :, piece * LANES:(piece + 1) * LANES] = acc.astype(BF16)

    n_big = float(Q) * lax.broadcasted_iota(jnp.int32, (pw_ref.shape[1], 1), 0).astype(F32)
    t_re, t_im = apow(n_big)
    pw_ref[0, :, :nst] = t_re
    pw_ref[0, :, nst:] = t_im


def _s5_tables(lam_re, lam_im, log_dt, b_re, b_im, c_re, c_im, d_skip):
    G, P = lam_re.shape
    C = b_re.shape[-1]
    nst = GPB * P
    eye = jnp.eye(GPB, dtype=bool)

    def expand(t):
        t = t.astype(F32).reshape(NGB, GPB, C, P)
        t = jnp.where(eye[None, :, None, :, None], t[:, :, :, None, :], 0.0)
        return t.reshape(NGB, GPB * C, nst)

    def lanes(t):
        return t.astype(F32).reshape(NGB, 1, nst)

    args = (lanes(lam_re), lanes(lam_im), lanes(jnp.broadcast_to(log_dt[:, None], (G, P))),
            expand(jnp.swapaxes(b_re, 1, 2)), expand(jnp.swapaxes(b_im, 1, 2)), expand(c_re), expand(c_im),
            d_skip.astype(F32).reshape(NGB, 1, LANES))
    row_spec = pl.BlockSpec((1, 1, nst), lambda gb: (gb, 0, 0))
    mat_spec = pl.BlockSpec((1, LANES, nst), lambda gb: (gb, 0, 0))
    n_pw = SEG + SUBLANES
    nsu = nst // NSUB
    return pl.pallas_call(
        _s5_prep_kernel,
        grid=(NGB,),
        in_specs=[row_spec, row_spec, row_spec, mat_spec, mat_spec, mat_spec, mat_spec,
                  pl.BlockSpec((1, 1, LANES), lambda gb: (gb, 0, 0))],
        out_specs=[
            pl.BlockSpec((1, NSUB, Q * SUBW, 2 * nsu), lambda gb: (gb, 0, 0, 0)),
            pl.BlockSpec((1, NSUB, Q * SUBW, 2 * nsu), lambda gb: (gb, 0, 0, 0)),
            pl.BlockSpec((1, NSUB, Q * SUBW, MXU_DIM), lambda gb: (gb, 0, 0, 0)),
            pl.BlockSpec((1, n_pw, 2 * nst), lambda gb: (gb, 0, 0)),
        ],
        out_shape=[
            jax.ShapeDtypeStruct((NGB, NSUB, Q * SUBW, 2 * nsu), BF16),
            jax.ShapeDtypeStruct((NGB, NSUB, Q * SUBW, 2 * nsu), BF16),
            jax.ShapeDtypeStruct((NGB, NSUB, Q * SUBW, MXU_DIM), BF16),
            jax.ShapeDtypeStruct((NGB, n_pw, 2 * nst), F32),
        ],
        scratch_shapes=[pltpu.VMEM((Q * LANES, 2 * nst), F32)],
        compiler_params=pltpu.CompilerParams(dimension_semantics=("arbitrary",), vmem_limit_bytes=VMEM_LIMIT),
        name="s5_prep",
    )(*args)


def _s5_layer(x, g, lam_re, lam_im, log_dt, b_re, b_im, c_re, c_im, d_skip, w_glu, b_glu, *, ma=64, mc=64):
    bsz, L, d = x.shape
    rows = bsz * L // Q
    rows_b = L // Q
    assert rows_b == NSEG * SEG and d == NGB * LANES
    w, vt, r_rev, pw = _s5_tables(lam_re, lam_im, log_dt, b_re, b_im, c_re, c_im, d_skip)
    xt = x.reshape(bsz * L, d)
    cp = pltpu.CompilerParams(dimension_semantics=("arbitrary",), vmem_limit_bytes=VMEM_LIMIT)

    h = pl.pallas_call(
        _s5_norm_kernel,
        grid=(rows // ma,),
        in_specs=[pl.BlockSpec((ma * Q, d), lambda i: (i, 0)), _const_spec((1, d))],
        out_specs=pl.BlockSpec((NGB, ma, Q * LANES), lambda i: (0, i, 0)),
        out_shape=jax.ShapeDtypeStruct((NGB, rows, Q * LANES), BF16),
        scratch_shapes=[pltpu.VMEM((NGB, ma * TOK_PITCH, LANES), F32)],
        compiler_params=cp,
        name="s5_norm",
    )(xt, g.reshape(1, d))

    nstate = pw.shape[-1]
    nslab = nstate // LANES
    y = pl.pallas_call(
        _s5_core_kernel,
        grid=(NGB, bsz),
        in_specs=[
            pl.BlockSpec((1, rows_b, Q * LANES), lambda gb, b: (gb, b, 0)),
            pl.BlockSpec((1,) + w.shape[1:], lambda gb, b: (gb, 0, 0, 0)),
            pl.BlockSpec((1,) + r_rev.shape[1:], lambda gb, b: (gb, 0, 0, 0)),
            pl.BlockSpec((1,) + vt.shape[1:], lambda gb, b: (gb, 0, 0, 0)),
            pl.BlockSpec((1, pw.shape[1], nstate), lambda gb, b: (gb, 0, 0)),
        ],
        out_specs=pl.BlockSpec((1, rows_b, Q * LANES), lambda gb, b: (gb, b, 0)),
        out_shape=jax.ShapeDtypeStruct((NGB, rows, Q * LANES), BF16),
        scratch_shapes=[
            pltpu.VMEM((nslab, NSEG * PITCH, LANES), F32),
            pltpu.VMEM((nslab, NSEG * PITCH, LANES), F32),
        ],
        compiler_params=pltpu.CompilerParams(
            dimension_semantics=("arbitrary", "arbitrary"), vmem_limit_bytes=VMEM_LIMIT),
        name="s5_core",
    )(h, w, r_rev, vt, pw)

    out = pl.pallas_call(
        _s5_out_kernel,
        grid=(rows // mc,),
        in_specs=[
            pl.BlockSpec((mc * Q, d), lambda i: (i, 0)),
            pl.BlockSpec((NGB, mc, Q * LANES), lambda i: (0, i, 0)),
            _const_spec((d, d)),
            _const_spec((1, d)),
        ],
        out_specs=pl.BlockSpec((mc * Q, d), lambda i: (i, 0)),
        out_shape=jax.ShapeDtypeStruct((bsz * L, d), F32),
        scratch_shapes=[pltpu.VMEM((NGB, mc * TOK_PITCH, LANES), F32)],
        compiler_params=cp,
        name="s5_out",
    )(xt, y, w_glu.astype(BF16), (0.5 * b_glu).reshape(1, d))
    return out.reshape(bsz, L, d)


def kernel(x, positions, norm_mix, norm_ffn, norm_final, s5_lambda_re, s5_lambda_im, s5_log_dt, s5_b_re, s5_b_im, s5_c_re, s5_c_im, s5_d, s5_w_glu, s5_b_glu, attn_w_qkv, attn_b_qkv, attn_sinks, attn_w_o, attn_b_o, ffn_w_up, ffn_w_conv, ffn_b_conv, ffn_w_down):
    depth = norm_mix.shape[0]
    w_up = ffn_w_up.astype(BF16)
    w_down = ffn_w_down.astype(BF16)
    for i in range(depth):
        j = i // 2
        if i % 2 == 0:
            x = _s5_layer(x, norm_mix[i], s5_lambda_re[j], s5_lambda_im[j], s5_log_dt[j],
                          s5_b_re[j], s5_b_im[j], s5_c_re[j], s5_c_im[j], s5_d[j], s5_w_glu[j], s5_b_glu[j])
        else:
            x = _swa_layer(x, positions, norm_mix[i], attn_w_qkv[j], attn_b_qkv[j], attn_sinks[j],
                           attn_w_o[j], attn_b_o[j])
        x = _conv_ffn(x, norm_ffn[i], w_up, ffn_w_conv[i], ffn_b_conv[i], w_down, i,
                      norm_final if i == depth - 1 else None)
    return x
```
